```python
import jax, jax.numpy as jnp
from jax import lax
import numpy as np

D_MODEL = 1024
BATCH = 32
SEQ = 2048
DEPTH = 2

PLE_DIM = 256
FOX_HEADS = 8
FOX_HEAD_DIM = 64
FOX_WIDTH = FOX_HEADS * FOX_HEAD_DIM
HG_HEADS = 4
HG_KEY_DIM = 128
HG_VAL_DIM = 128
HG_WIDTH = HG_HEADS * HG_VAL_DIM
MIX_WIDTH = FOX_WIDTH + HG_WIDTH
IN_COLS = 3 * FOX_WIDTH + FOX_HEADS + 4 * HG_WIDTH
QUERY_BLOCK = 128
HG_CHUNK = 32
PEER_HEADS = 8
PEER_N_KEYS = 128
PEER_N_EXPERTS = PEER_N_KEYS * PEER_N_KEYS
PEER_QUERY_DIM = 256
PEER_HALF_DIM = PEER_QUERY_DIM // 2
PEER_TOPK = 16
PEER_TOKEN_BLOCK = 128
DEEPNORM_ALPHA = (2 * DEPTH) ** 0.25
DEEPNORM_BETA = (8 * DEPTH) ** -0.25
LN_EPS = 1e-5
RMS_EPS = 1e-6
MASK_VALUE = -1e30
LB_FLOOR = 1e-30

kernel_name = 'hybrid_fox_hgrn2_peer_deepnorm'


def layer_norm(x, g, b):
    xf = x.astype(jnp.float32)
    mu = xf.mean(-1, keepdims=True)
    var = jnp.square(xf - mu).mean(-1, keepdims=True)
    return ((xf - mu) * lax.rsqrt(var + LN_EPS) * g + b).astype(x.dtype)


def rms_norm(x, g):
    xf = x.astype(jnp.float32)
    return (xf * lax.rsqrt(jnp.square(xf).mean(-1, keepdims=True) + RMS_EPS) * g).astype(x.dtype)


def fox_attention(q, k, v, log_f):
    T = q.shape[1]
    c = jnp.cumsum(log_f, axis=1).transpose(0, 2, 1)
    scale = FOX_HEAD_DIM ** -0.5
    outs = []
    for blk in range(T // QUERY_BLOCK):
        q0 = blk * QUERY_BLOCK
        q1 = q0 + QUERY_BLOCK
        s = jnp.einsum('bqhd,bkhd->bhqk', q[:, q0:q1], k[:, :q1]).astype(jnp.float32) * scale
        s = s + c[:, :, q0:q1, None] - c[:, :, None, :q1]
        mask = jnp.arange(q1)[None, :] <= jnp.arange(q0, q1)[:, None]
        pr = jax.nn.softmax(jnp.where(mask, s, MASK_VALUE), axis=-1)
        outs.append(jnp.einsum('bhqk,bkhd->bqhd', pr.astype(v.dtype), v[:, :q1]))
    return jnp.concatenate(outs, axis=1)


def hgrn2_chunkwise(q, k, log_f, v):
    B, T, H, dk = q.shape
    dv = v.shape[-1]
    n_chunks = T // HG_CHUNK

    def to_chunks(a):
        return a.reshape(B, n_chunks, HG_CHUNK, H, a.shape[-1]).transpose(1, 0, 3, 2, 4)

    causal = jnp.tril(jnp.ones((HG_CHUNK, HG_CHUNK), dtype=bool))

    def step(S, inp):
        qc, kc, lfc, vc = inp
        b = jnp.cumsum(lfc, axis=2)
        o_inter = jnp.einsum('bhtk,bhkv->bhtv', qc * jnp.exp(b), S)
        diff = b[:, :, :, None, :] - b[:, :, None, :, :]
        decay = jnp.exp(jnp.where(causal[:, :, None], diff, MASK_VALUE))
        A = jnp.einsum('bhtk,bhsk,bhtsk->bhts', qc, kc, decay)
        o_intra = jnp.einsum('bhts,bhsv->bhtv', A, vc)
        b_last = b[:, :, -1:, :]
        S_new = jnp.exp(b_last[:, :, 0, :, None]) * S + jnp.einsum('bhsk,bhsv->bhkv', kc * jnp.exp(b_last - b), vc)
        return S_new, o_inter + o_intra

    S0 = jnp.zeros((B, H, dk, dv), jnp.float32)
    _, o = lax.scan(step, S0, (to_chunks(q), to_chunks(k), to_chunks(log_f), to_chunks(v)))
    return o.transpose(1, 0, 3, 2, 4).reshape(B, T, H, dv)


def hybrid_mixer(h, w_in, fox_fb, fox_norm_g, lower_bound, hg_norm_g, w_out):
    B, T, _ = h.shape
    proj = h @ w_in
    split_at = [int(s) for s in np.cumsum([FOX_WIDTH, FOX_WIDTH, FOX_WIDTH, FOX_HEADS, HG_WIDTH, HG_WIDTH, HG_WIDTH])]
    fq, fk, fv, ff, hq, hf, hi, hg = jnp.split(proj, split_at, axis=-1)
    fq = fq.reshape(B, T, FOX_HEADS, FOX_HEAD_DIM)
    fk = fk.reshape(B, T, FOX_HEADS, FOX_HEAD_DIM)
    fv = fv.reshape(B, T, FOX_HEADS, FOX_HEAD_DIM)
    log_f_fox = jax.nn.log_sigmoid((ff + fox_fb).astype(jnp.float32))
    fox_o = fox_attention(fq, fk, fv, log_f_fox).reshape(B, T, FOX_WIDTH)
    fox_o = rms_norm(fox_o, fox_norm_g)
    z = hf.astype(jnp.float32).reshape(B, T, HG_HEADS, HG_KEY_DIM)
    lb = lower_bound.reshape(HG_HEADS, HG_KEY_DIM)
    log_f_hg = jnp.logaddexp(jnp.log(jnp.maximum(lb, LB_FLOOR)), jnp.log1p(-lb) + jax.nn.log_sigmoid(z))
    k_hg = (1.0 - lb) * jax.nn.sigmoid(-z)
    o_hg = hgrn2_chunkwise(hq.astype(jnp.float32).reshape(B, T, HG_HEADS, HG_KEY_DIM), k_hg, log_f_hg,
                           hi.astype(jnp.float32).reshape(B, T, HG_HEADS, HG_VAL_DIM))
    o_hg = rms_norm(o_hg, hg_norm_g.reshape(HG_HEADS, HG_VAL_DIM)).reshape(B, T, HG_WIDTH)
    o_hg = (o_hg * jax.nn.silu(hg.astype(jnp.float32))).astype(h.dtype)
    return jnp.concatenate([fox_o, o_hg], axis=-1) @ w_out


def peer_ffn(h, wq, keys, u, v):
    B, T, D = h.shape
    xt = h.reshape(-1, PEER_TOKEN_BLOCK, D)

    def block(xb):
        q = (xb @ wq).reshape(PEER_TOKEN_BLOCK, PEER_HEADS, 2, PEER_HALF_DIM)
        s = jnp.einsum('thpd,hpnd->thpn', q, keys).astype(jnp.float32)
        s1, i1 = lax.top_k(s[:, :, 0], PEER_TOPK)
        s2, i2 = lax.top_k(s[:, :, 1], PEER_TOPK)
        cand_s = (s1[..., :, None] + s2[..., None, :]).reshape(PEER_TOKEN_BLOCK, PEER_HEADS, PEER_TOPK * PEER_TOPK)
        cand_i = (i1[..., :, None] * PEER_N_KEYS + i2[..., None, :]).reshape(PEER_TOKEN_BLOCK, PEER_HEADS, PEER_TOPK * PEER_TOPK)
        top_s, pos = lax.top_k(cand_s, PEER_TOPK)
        idx = jnp.take_along_axis(cand_i, pos, axis=-1)
        gate = jax.nn.softmax(top_s, axis=-1)
        act = jax.nn.gelu(jnp.einsum('thkd,td->thk', u[idx], xb).astype(jnp.float32), approximate=False)
        return jnp.einsum('thk,thkd->td', (gate * act).astype(xb.dtype), v[idx])

    return lax.map(block, xt).reshape(B, T, D)


def setup_inputs(seed: int = 0) -> dict:
    key = jax.random.key(seed)
    ks = jax.random.split(key, 24)

    def nrm(k, shape, std):
        return std * jax.random.normal(k, shape, jnp.float32)

    beta = DEEPNORM_BETA
    col_scale = jnp.concatenate([
        jnp.ones((2 * FOX_WIDTH,), jnp.float32),
        jnp.full((FOX_WIDTH,), beta, jnp.float32),
        jnp.ones((FOX_HEADS + 2 * HG_WIDTH,), jnp.float32),
        jnp.full((HG_WIDTH,), beta, jnp.float32),
        jnp.ones((HG_WIDTH,), jnp.float32)])
    return {
        'x': nrm(ks[0], (BATCH, SEQ, D_MODEL), 1.0),
        'p': nrm(ks[1], (DEPTH, BATCH, SEQ, PLE_DIM), 1.0),
        'emb_ln_g': 1.0 + nrm(ks[2], (D_MODEL,), 0.02),
        'emb_ln_b': nrm(ks[3], (D_MODEL,), 0.02),
        'w_in': nrm(ks[4], (DEPTH, D_MODEL, IN_COLS), D_MODEL ** -0.5) * col_scale,
        'fox_fb': 1.0 + nrm(ks[5], (DEPTH, FOX_HEADS), 0.5),
        'fox_norm_g': 1.0 + nrm(ks[6], (DEPTH, FOX_WIDTH), 0.02),
        'hg_lb_logits': nrm(ks[7], (DEPTH, HG_HEADS * HG_KEY_DIM), 0.5),
        'hg_norm_g': 1.0 + nrm(ks[8], (DEPTH, HG_WIDTH), 0.02),
        'w_out': nrm(ks[9], (DEPTH, MIX_WIDTH, D_MODEL), beta * MIX_WIDTH ** -0.5),
        'ln1_g': 1.0 + nrm(ks[10], (DEPTH, D_MODEL), 0.02),
        'ln1_b': nrm(ks[11], (DEPTH, D_MODEL), 0.02),
        'peer_wq': nrm(ks[12], (DEPTH, D_MODEL, PEER_HEADS * PEER_QUERY_DIM), D_MODEL ** -0.5),
        'peer_keys': nrm(ks[13], (DEPTH, PEER_HEADS, 2, PEER_N_KEYS, PEER_HALF_DIM), PEER_HALF_DIM ** -0.5),
        'peer_u': nrm(ks[14], (DEPTH, PEER_N_EXPERTS, D_MODEL), D_MODEL ** -0.5),
        'peer_v': nrm(ks[15], (DEPTH, PEER_N_EXPERTS, D_MODEL), beta * PEER_HEADS ** -0.5),
        'ple_gate_w': nrm(ks[16], (DEPTH, D_MODEL, D_MODEL), D_MODEL ** -0.5),
        'ple_w': nrm(ks[17], (DEPTH, PLE_DIM, D_MODEL), beta * PLE_DIM ** -0.5),
        'ln2_g': 1.0 + nrm(ks[18], (DEPTH, D_MODEL), 0.02),
        'ln2_b': nrm(ks[19], (DEPTH, D_MODEL), 0.02),
    }


def reference(x, p, emb_ln_g, emb_ln_b, w_in, fox_fb, fox_norm_g, hg_lb_logits, hg_norm_g, w_out,
              ln1_g, ln1_b, peer_wq, peer_keys, peer_u, peer_v, ple_gate_w, ple_w, ln2_g, ln2_b):
    lb_probs = jax.nn.softmax(hg_lb_logits.astype(jnp.float32), axis=0)
    lower_bounds = jnp.cumsum(lb_probs, axis=0) - lb_probs[:1]
    h = layer_norm(x, emb_ln_g, emb_ln_b)
    for i in range(DEPTH):
        mix = hybrid_mixer(h, w_in[i], fox_fb[i], fox_norm_g[i], lower_bounds[i], hg_norm_g[i], w_out[i])
        h = layer_norm(DEEPNORM_ALPHA * h + mix, ln1_g[i], ln1_b[i])
        ffn = peer_ffn(h, peer_wq[i], peer_keys[i], peer_u[i], peer_v[i])
        ple = jax.nn.sigmoid(h @ ple_gate_w[i]) * (p[i] @ ple_w[i])
        h = layer_norm(DEEPNORM_ALPHA * h + ffn + ple, ln2_g[i], ln2_b[i])
    return h
```

```python
import functools

import jax
import jax.numpy as jnp
from jax import lax
from jax.experimental import pallas as pl
from jax.experimental.pallas import tpu as pltpu

D_MODEL = 1024
DEPTH = 2
PLE_DIM = 256
FOX_HEADS = 8
FOX_HEAD_DIM = 64
FOX_WIDTH = FOX_HEADS * FOX_HEAD_DIM
HG_HEADS = 4
HG_KEY_DIM = 128
HG_VAL_DIM = 128
HG_WIDTH = HG_HEADS * HG_VAL_DIM
PEER_HEADS = 8
PEER_N_KEYS = 128
PEER_N_EXPERTS = PEER_N_KEYS * PEER_N_KEYS
PEER_HALF_DIM = 128
PEER_TOPK = 16
PEER_SLOTS = PEER_HEADS * PEER_TOPK
DEEPNORM_ALPHA = (2 * DEPTH) ** 0.25
LN_EPS = 1e-5
RMS_EPS = 1e-6
MASK_VALUE = -1e30
LB_FLOOR = 1e-30

LANES = 128
SUBLANES = 8
VMEM_LIMIT_BYTES = 56 * 1024 * 1024

HG_CHUNK = 16
HG_CUMSUM_ROWS = 128
W_ROW_PITCH = PEER_N_KEYS + SUBLANES

F32 = jnp.float32
BF16 = jnp.bfloat16


def _params(sem):
    return pltpu.CompilerParams(dimension_semantics=sem, vmem_limit_bytes=VMEM_LIMIT_BYTES)


def _dot(a, b):
    return jnp.dot(a, b, preferred_element_type=F32)


def _dot_nt(a, b):
    return lax.dot_general(a, b, (((1,), (1,)), ((), ())), preferred_element_type=F32)


def _layer_norm_rows(y, g, b):
    mu = jnp.mean(y, axis=-1, keepdims=True)
    d = y - mu
    var = jnp.mean(d * d, axis=-1, keepdims=True)
    return d * lax.rsqrt(var + LN_EPS) * g + b


def _ln_kernel(x_ref, g_ref, b_ref, h_ref, hb_ref):
    h = _layer_norm_rows(x_ref[...], g_ref[...], b_ref[...])
    h_ref[...] = h
    hb_ref[...] = h.astype(BF16)


def _ln_call(x, g, b, tm):
    n, d = x.shape
    return pl.pallas_call(
        _ln_kernel,
        grid=(n // tm,),
        in_specs=[pl.BlockSpec((tm, d), lambda i: (i, 0)),
                  pl.BlockSpec((1, d), lambda i: (0, 0)),
                  pl.BlockSpec((1, d), lambda i: (0, 0))],
        out_specs=[pl.BlockSpec((tm, d), lambda i: (i, 0)),
                   pl.BlockSpec((tm, d), lambda i: (i, 0))],
        out_shape=[jax.ShapeDtypeStruct((n, d), F32), jax.ShapeDtypeStruct((n, d), BF16)],
        compiler_params=_params(("parallel",)),
        name="emb_ln",
    )(x, g.reshape(1, d), b.reshape(1, d))


def _inproj_kernel(h_ref, wqkv_ref, wff_ref, whg_ref, qkv_ref, ff_ref, hg_ref):
    h = h_ref[...]
    qkv_ref[...] = _dot(h, wqkv_ref[...]).astype(BF16)
    ff_ref[...] = _dot(h, wff_ref[...])
    hg_ref[...] = _dot(h, whg_ref[...])


def _inproj_call(hb, wqkv, wff, whg, tm):
    n, d = hb.shape
    return pl.pallas_call(
        _inproj_kernel,
        grid=(n // tm,),
        in_specs=[pl.BlockSpec((tm, d), lambda i: (i, 0)),
                  pl.BlockSpec(wqkv.shape, lambda i: (0, 0)),
                  pl.BlockSpec(wff.shape, lambda i: (0, 0)),
                  pl.BlockSpec(whg.shape, lambda i: (0, 0))],
        out_specs=[pl.BlockSpec((tm, wqkv.shape[1]), lambda i: (i, 0)),
                   pl.BlockSpec((tm, wff.shape[1]), lambda i: (i, 0)),
                   pl.BlockSpec((tm, whg.shape[1]), lambda i: (i, 0))],
        out_shape=[jax.ShapeDtypeStruct((n, wqkv.shape[1]), BF16),
                   jax.ShapeDtypeStruct((n, wff.shape[1]), F32),
                   jax.ShapeDtypeStruct((n, whg.shape[1]), F32)],
        compiler_params=_params(("parallel",)),
        name="in_proj",
    )(hb, wqkv, wff, whg)


def _log_sigmoid(z):
    return jnp.minimum(z, 0.0) - jnp.log(1.0 + jnp.exp(-jnp.abs(z)))


def _fox_cumsum_kernel(ff_ref, fb_ref, c_ref):
    rows, t = ff_ref.shape
    lf = _log_sigmoid(ff_ref[...] + fb_ref[...])
    r = lax.broadcasted_iota(jnp.int32, (LANES, LANES), 0)
    c = lax.broadcasted_iota(jnp.int32, (LANES, LANES), 1)
    upper = (r <= c).astype(F32)
    carry = jnp.zeros((rows, 1), F32)
    for j in range(t // LANES):
        blk = lf[:, j * LANES:(j + 1) * LANES]
        cs = jnp.dot(blk, upper, preferred_element_type=F32, precision=lax.Precision.HIGHEST)
        cs = cs + carry
        c_ref[:, j * LANES:(j + 1) * LANES] = cs
        carry = cs[:, LANES - 1:LANES]


def _fox_cumsum_call(ff_t, fb_rows):
    rows, t = ff_t.shape
    tr = SUBLANES
    return pl.pallas_call(
        _fox_cumsum_kernel,
        grid=(rows // tr,),
        in_specs=[pl.BlockSpec((tr, t), lambda i: (i, 0)),
                  pl.BlockSpec((tr, 1), lambda i: (i, 0))],
        out_specs=pl.BlockSpec((tr, t), lambda i: (i, 0)),
        out_shape=jax.ShapeDtypeStruct((rows, t), F32),
        compiler_params=_params(("parallel",)),
        name="fox_cumsum",
    )(ff_t, fb_rows)


def _fox_kernel(q_ref, k_ref, v_ref, c_ref, o_ref, *, tq, tk):
    qi = pl.program_id(2)
    q = q_ref[...]
    lane = lax.broadcasted_iota(jnp.int32, (1, LANES), 1)
    head_lanes = [lane < FOX_HEAD_DIM, lane >= FOX_HEAD_DIM]
    zero = jnp.zeros_like(q)
    qh = [jnp.where(head_lanes[0], q, zero), jnp.where(head_lanes[1], q, zero)]
    q0 = pl.multiple_of(qi * tq, tq)
    cq = c_ref[:, pl.ds(q0, tq)]
    ct = [jnp.transpose(jnp.broadcast_to(cq[h:h + 1, :], (LANES, tq))) for h in range(2)]
    reps = tk // LANES
    ct = [jnp.concatenate([x] * reps, axis=1) if reps > 1 else x for x in ct]

    def block(kj, carry, masked):
        k0 = pl.multiple_of(kj * tk, tk)
        kb = k_ref[pl.ds(k0, tk), :]
        vb = v_ref[pl.ds(k0, tk), :]
        cs = c_ref[:, pl.ds(k0, tk)]
        out = []
        for h in range(2):
            m_old, acc_old = carry[h]
            s = _dot_nt(qh[h], kb)
            s = s + (ct[h] - cs[h:h + 1, :])
            if masked:
                row = lax.broadcasted_iota(jnp.int32, (tq, tk), 0) + q0
                col = lax.broadcasted_iota(jnp.int32, (tq, tk), 1) + k0
                s = jnp.where(col <= row, s, MASK_VALUE)
            m_new = jnp.maximum(m_old, jnp.max(s, axis=-1, keepdims=True))
            alpha = jnp.exp(m_old - m_new)
            p = jnp.exp(s - m_new).astype(BF16)
            vh = jnp.where(head_lanes[h], vb, jnp.ones_like(vb))
            acc_new = alpha * acc_old + _dot(p, vh)
            out.append((m_new, acc_new))
        return tuple(out)

    init = tuple((jnp.full((tq, 1), MASK_VALUE, F32), jnp.zeros((tq, LANES), F32)) for _ in range(2))
    n_full = (qi * tq) // tk
    carry = lax.fori_loop(0, n_full, lambda kj, c: block(kj, c, False), init)
    for j in range(tq // tk):
        carry = block(n_full + j, carry, True)
    res = []
    for h in range(2):
        acc = carry[h][1]
        denom = pltpu.roll(acc, FOX_HEAD_DIM, axis=1)
        res.append(acc / denom)
    o_ref[...] = jnp.where(head_lanes[0], res[0], res[1])


def _fox_call(qkv, c, batch, seq, tq, tk):
    pairs = FOX_HEADS // 2
    return pl.pallas_call(
        functools.partial(_fox_kernel, tq=tq, tk=tk),
        grid=(batch, pairs, seq // tq),
        in_specs=[pl.BlockSpec((None, tq, LANES), lambda b, p, i: (b, i, p)),
                  pl.BlockSpec((None, seq, LANES), lambda b, p, i: (b, 0, pairs + p)),
                  pl.BlockSpec((None, seq, LANES), lambda b, p, i: (b, 0, 2 * pairs + p)),
                  pl.BlockSpec((None, None, 2, seq), lambda b, p, i: (b, p, 0, 0))],
        out_specs=pl.BlockSpec((None, tq, LANES), lambda b, p, i: (b, i, p)),
        out_shape=jax.ShapeDtypeStruct((batch, seq, FOX_WIDTH), F32),
        compiler_params=_params(("parallel", "parallel", "arbitrary")),
        name="fox_attention",
    )(qkv, qkv, qkv, c)


def _hgrn_kernel(q_ref, z_ref, v_ref, lbl_ref, o_ref, lf_ref, kk_ref, st_ref, *, seq):
    logits = lbl_ref[...]
    lg = logits[:DEPTH, :]
    mx = jnp.max(lg, axis=0, keepdims=True)
    ex = jnp.exp(lg - mx)
    probs = ex / jnp.sum(ex, axis=0, keepdims=True)
    sel = logits[DEPTH:DEPTH + 1, :]
    lb = jnp.zeros_like(sel)
    run = jnp.zeros_like(sel)
    for i in range(DEPTH):
        run = run + probs[i:i + 1, :]
        lb = jnp.where(sel == float(i), run - probs[0:1, :], lb)
    log_lb = jnp.log(jnp.maximum(lb, LB_FLOOR))
    log_1m = jnp.log(1.0 - lb)

    rr = lax.broadcasted_iota(jnp.int32, (HG_CUMSUM_ROWS, HG_CUMSUM_ROWS), 0)
    cc = lax.broadcasted_iota(jnp.int32, (HG_CUMSUM_ROWS, HG_CUMSUM_ROWS), 1)
    tri = ((rr // HG_CHUNK == cc // HG_CHUNK) & (cc <= rr)).astype(F32)

    def prep(j, _):
        r0 = pl.multiple_of(j * HG_CUMSUM_ROWS, HG_CUMSUM_ROWS)
        z = z_ref[pl.ds(r0, HG_CUMSUM_ROWS), :]
        a = log_1m + _log_sigmoid(z)
        mm = jnp.maximum(log_lb, a)
        lf = mm + jnp.log(1.0 + jnp.exp(-jnp.abs(log_lb - a)))
        lf_ref[pl.ds(r0, HG_CUMSUM_ROWS), :] = jnp.dot(
            tri, lf, preferred_element_type=F32, precision=lax.Precision.HIGHEST)
        kk_ref[pl.ds(r0, HG_CUMSUM_ROWS), :] = (1.0 - lb) * jax.nn.sigmoid(-z)
        return 0

    lax.fori_loop(0, seq // HG_CUMSUM_ROWS, prep, 0)
    st_ref[...] = jnp.zeros_like(st_ref)
    row = lax.broadcasted_iota(jnp.int32, (HG_CHUNK, 1), 0)

    def chunk(ci, _):
        r0 = pl.multiple_of(ci * HG_CHUNK, HG_CHUNK)
        b = lf_ref[pl.ds(r0, HG_CHUNK), :]
        q = q_ref[pl.ds(r0, HG_CHUNK), :]
        kk = kk_ref[pl.ds(r0, HG_CHUNK), :]
        v = v_ref[pl.ds(r0, HG_CHUNK), :]
        b_last = b[HG_CHUNK - 1:HG_CHUNK, :]
        qe = (q * jnp.exp(b)).astype(BF16)
        kd = (kk * jnp.exp(b_last - b)).astype(BF16)
        e_last = jnp.exp(b_last)
        outs = []
        for h in range(HG_HEADS):
            sl = slice(h * HG_KEY_DIM, (h + 1) * HG_KEY_DIM)
            st = st_ref[h]
            o = _dot_nt(qe[:, sl], st.astype(BF16))
            bh, qh, kh, vh = b[:, sl], q[:, sl], kk[:, sl], v[:, sl]
            for s in range(HG_CHUNK):
                d = jnp.exp(jnp.where(row >= s, bh - bh[s:s + 1, :], MASK_VALUE))
                a_col = jnp.sum(qh * kh[s:s + 1, :] * d, axis=-1, keepdims=True)
                o = o + a_col * vh[s:s + 1, :]
            upd = lax.dot_general(vh.astype(BF16), kd[:, sl], (((0,), (0,)), ((), ())),
                                  preferred_element_type=F32)
            st_ref[h] = st * e_last[:, sl] + upd
            outs.append(o)
        o_ref[pl.ds(r0, HG_CHUNK), :] = jnp.concatenate(outs, axis=1)
        return 0

    lax.fori_loop(0, seq // HG_CHUNK, chunk, 0)


def _hgrn_call(hgp, lbl, batch, seq):
    return pl.pallas_call(
        functools.partial(_hgrn_kernel, seq=seq),
        grid=(batch,),
        in_specs=[pl.BlockSpec((None, seq, HG_WIDTH), lambda b: (b, 0, 0)),
                  pl.BlockSpec((None, seq, HG_WIDTH), lambda b: (b, 0, 1)),
                  pl.BlockSpec((None, seq, HG_WIDTH), lambda b: (b, 0, 2)),
                  pl.BlockSpec(lbl.shape, lambda b: (0, 0))],
        out_specs=pl.BlockSpec((None, seq, HG_WIDTH), lambda b: (b, 0, 0)),
        out_shape=jax.ShapeDtypeStruct((batch, seq, HG_WIDTH), F32),
        scratch_shapes=[pltpu.VMEM((seq, HG_WIDTH), F32),
                        pltpu.VMEM((seq, HG_WIDTH), F32),
                        pltpu.VMEM((HG_HEADS, HG_VAL_DIM, HG_KEY_DIM), F32)],
        compiler_params=_params(("parallel",)),
        name="hgrn2",
    )(hgp, hgp, hgp, lbl)


def _mix_kernel(fox_ref, ohg_ref, gate_ref, h_ref, gf_ref, gh_ref, wo_ref, g1_ref, b1_ref,
                h1_ref, h1b_ref):
    fox = fox_ref[...]
    fox = fox * lax.rsqrt(jnp.mean(fox * fox, axis=-1, keepdims=True) + RMS_EPS) * gf_ref[...]
    ohg = ohg_ref[...]
    gh = gh_ref[...]
    gate = gate_ref[...]
    parts = [fox.astype(BF16)]
    for h in range(HG_HEADS):
        sl = slice(h * HG_VAL_DIM, (h + 1) * HG_VAL_DIM)
        o = ohg[:, sl]
        o = o * lax.rsqrt(jnp.mean(o * o, axis=-1, keepdims=True) + RMS_EPS) * gh[:, sl]
        g = gate[:, sl]
        parts.append((o * (g * jax.nn.sigmoid(g))).astype(BF16))
    cat = jnp.concatenate(parts, axis=1)
    mix = _dot(cat, wo_ref[...])
    h1 = _layer_norm_rows(DEEPNORM_ALPHA * h_ref[...] + mix, g1_ref[...], b1_ref[...])
    h1_ref[...] = h1
    h1b_ref[...] = h1.astype(BF16)


def _mix_call(fox_o, ohg, hgp, h, gf, gh, wo, g1, b1, tm):
    n, d = h.shape
    row = lambda a: a.reshape(1, -1)
    return pl.pallas_call(
        _mix_kernel,
        grid=(n // tm,),
        in_specs=[pl.BlockSpec((tm, FOX_WIDTH), lambda i: (i, 0)),
                  pl.BlockSpec((tm, HG_WIDTH), lambda i: (i, 0)),
                  pl.BlockSpec((tm, HG_WIDTH), lambda i: (i, 3)),
                  pl.BlockSpec((tm, d), lambda i: (i, 0)),
                  pl.BlockSpec((1, FOX_WIDTH), lambda i: (0, 0)),
                  pl.BlockSpec((1, HG_WIDTH), lambda i: (0, 0)),
                  pl.BlockSpec(wo.shape, lambda i: (0, 0)),
                  pl.BlockSpec((1, d), lambda i: (0, 0)),
                  pl.BlockSpec((1, d), lambda i: (0, 0))],
        out_specs=[pl.BlockSpec((tm, d), lambda i: (i, 0)),
                   pl.BlockSpec((tm, d), lambda i: (i, 0))],
        out_shape=[jax.ShapeDtypeStruct((n, d), F32), jax.ShapeDtypeStruct((n, d), BF16)],
        compiler_params=_params(("parallel",)),
        name="mix_out_ln1",
    )(fox_o, ohg, hgp, h, row(gf), row(gh), wo, row(g1), row(b1))


def _colmax(x):
    return jnp.max(x, axis=0, keepdims=True)


def _colmin(x):
    return jnp.min(x, axis=0, keepdims=True)


def _top16_rows(s, val_ref, idx_ref):
    n = s.shape[0]
    rows = lax.broadcasted_iota(jnp.int32, s.shape, 0)

    def rnd(r, s):
        m = _colmax(s)
        i = _colmin(jnp.where(s == m, rows, n))
        val_ref[pl.ds(r, 1), :] = m
        idx_ref[pl.ds(r, 1), :] = i
        return jnp.where(rows == i, -jnp.inf, s)

    lax.fori_loop(0, PEER_TOPK, rnd, s)


def _cand_counts():
    return [PEER_TOPK // (j1 + 1) for j1 in range(PEER_TOPK)]


def _peerq_kernel(h_ref, wq_ref, keys_ref, a_ref, b_ref, g_ref,
                  v1_ref, i1_ref, v2_ref, i2_ref, cv_ref, ci_ref, oa_ref, ob_ref, og_ref):
    tb = h_ref.shape[0]
    q = _dot(h_ref[...], wq_ref[...]).astype(BF16)
    counts = _cand_counts()
    for h in range(PEER_HEADS):
        for p, (vr, ir) in enumerate(((v1_ref, i1_ref), (v2_ref, i2_ref))):
            c0 = (h * 2 + p) * PEER_HALF_DIM
            s = _dot_nt(keys_ref[h * 2 + p], q[:, c0:c0 + PEER_HALF_DIM])
            _top16_rows(s, vr, ir)
        s1 = v1_ref[...]
        s2 = v2_ref[...]
        blocks, flat = [], []
        for j1 in range(PEER_TOPK):
            rows = PEER_TOPK if counts[j1] > SUBLANES else SUBLANES
            blocks.append(s1[j1:j1 + 1, :] + s2[:rows, :])
            flat.append(lax.broadcasted_iota(jnp.int32, (rows, tb), 0) + j1 * PEER_TOPK)
        cand = jnp.concatenate(blocks, axis=0)
        flat = jnp.concatenate(flat, axis=0)
        big = PEER_TOPK * PEER_TOPK

        def rnd(r, cand):
            m = _colmax(cand)
            f = _colmin(jnp.where(cand == m, flat, big))
            cv_ref[pl.ds(r, 1), :] = m
            ci_ref[pl.ds(r, 1), :] = f
            return jnp.where(flat == f, -jnp.inf, cand)

        lax.fori_loop(0, PEER_TOPK, rnd, cand)
        top_s = cv_ref[...]
        f = ci_ref[...]
        j1 = lax.shift_right_logical(f, 4)
        j2 = f & (PEER_TOPK - 1)
        i1 = i1_ref[...]
        i2 = i2_ref[...]
        a = jnp.zeros_like(f)
        b = jnp.zeros_like(f)
        for j in range(PEER_TOPK):
            a = jnp.where(j1 == j, i1[j:j + 1, :], a)
            b = jnp.where(j2 == j, i2[j:j + 1, :], b)
        e = jnp.exp(top_s - top_s[0:1, :])
        gate = e / jnp.sum(e, axis=0, keepdims=True)
        oa_ref[h * PEER_TOPK:(h + 1) * PEER_TOPK, :] = a
        ob_ref[h * PEER_TOPK:(h + 1) * PEER_TOPK, :] = b
        og_ref[h * PEER_TOPK:(h + 1) * PEER_TOPK, :] = gate
    a_ref[...] = jnp.transpose(oa_ref[...])
    b_ref[...] = jnp.transpose(ob_ref[...])
    g_ref[...] = jnp.transpose(og_ref[...])


def _peerq_call(h1b, wq, keys, tb):
    n, d = h1b.shape
    slot = lambda dt: jax.ShapeDtypeStruct((n, PEER_SLOTS), dt)
    return pl.pallas_call(
        _peerq_kernel,
        grid=(n // tb,),
        in_specs=[pl.BlockSpec((tb, d), lambda i: (i, 0)),
                  pl.BlockSpec(wq.shape, lambda i: (0, 0)),
                  pl.BlockSpec(keys.shape, lambda i: (0, 0, 0))],
        out_specs=[pl.BlockSpec((tb, PEER_SLOTS), lambda i: (i, 0))] * 3,
        out_shape=[slot(jnp.int32), slot(jnp.int32), slot(F32)],
        scratch_shapes=[pltpu.VMEM((PEER_TOPK, tb), F32), pltpu.VMEM((PEER_TOPK, tb), jnp.int32),
                        pltpu.VMEM((PEER_TOPK, tb), F32), pltpu.VMEM((PEER_TOPK, tb), jnp.int32),
                        pltpu.VMEM((PEER_TOPK, tb), F32), pltpu.VMEM((PEER_TOPK, tb), jnp.int32),
                        pltpu.VMEM((PEER_SLOTS, tb), jnp.int32), pltpu.VMEM((PEER_SLOTS, tb), jnp.int32),
                        pltpu.VMEM((PEER_SLOTS, tb), F32)],
        compiler_params=_params(("parallel",)),
        name="peer_retrieve",
    )(h1b, wq, keys)


def _peer_act_kernel(x_ref, ut_ref, a_ref, b_ref, o_ref, *, ec):
    j = pl.program_id(1)

    @pl.when(j == 0)
    def _():
        o_ref[...] = jnp.zeros_like(o_ref)

    act = _dot(x_ref[...], ut_ref[...])
    a_idx = a_ref[...]
    b_idx = b_ref[...]
    acc = o_ref[...]
    groups = ec // PEER_N_KEYS
    for g in range(groups):
        i1 = j * groups + g
        picked = jnp.take_along_axis(act[:, g * PEER_N_KEYS:(g + 1) * PEER_N_KEYS], b_idx, axis=1)
        acc = jnp.where(a_idx == i1, picked, acc)
    o_ref[...] = acc


def _peer_act_call(h1b, ut, a_idx, b_idx, tb, ec):
    n, d = h1b.shape
    return pl.pallas_call(
        functools.partial(_peer_act_kernel, ec=ec),
        grid=(n // tb, PEER_N_EXPERTS // ec),
        in_specs=[pl.BlockSpec((tb, d), lambda i, j: (i, 0)),
                  pl.BlockSpec((d, ec), lambda i, j: (0, j)),
                  pl.BlockSpec((tb, PEER_SLOTS), lambda i, j: (i, 0)),
                  pl.BlockSpec((tb, PEER_SLOTS), lambda i, j: (i, 0))],
        out_specs=pl.BlockSpec((tb, PEER_SLOTS), lambda i, j: (i, 0)),
        out_shape=jax.ShapeDtypeStruct((n, PEER_SLOTS), F32),
        compiler_params=_params(("parallel", "arbitrary")),
        name="peer_expert_in",
    )(h1b, ut, a_idx, b_idx)


def _peer_w_kernel(act_ref, g_ref, a_ref, b_ref, w_ref, wt_ref, w3_ref):
    tb = act_ref.shape[0]
    act = act_ref[...]
    gelu = 0.5 * act * (1.0 + lax.erf(act * (2.0 ** -0.5)))
    wt_ref[...] = g_ref[...] * gelu
    sub = lax.broadcasted_iota(jnp.int32, (PEER_N_KEYS, PEER_SLOTS), 0)

    def token(t, _):
        a_row = a_ref[pl.ds(t, 1), :]
        b_row = b_ref[pl.ds(t, 1), :]
        w_row = wt_ref[pl.ds(t, 1), :]
        pt = jnp.where(sub == a_row, w_row, 0.0).astype(BF16)
        qt = jnp.where(sub == b_row, 1.0, 0.0).astype(BF16)
        r0 = pl.multiple_of(t * W_ROW_PITCH, SUBLANES)
        w3_ref[pl.ds(r0, PEER_N_KEYS), :] = _dot_nt(pt, qt)
        return 0

    lax.fori_loop(0, tb, token, 0)
    for i1 in range(PEER_N_KEYS):
        rows = w3_ref[pl.ds(i1, tb, stride=W_ROW_PITCH), :]
        w_ref[:, i1 * PEER_N_KEYS:(i1 + 1) * PEER_N_KEYS] = rows.astype(BF16)


def _peer_w_call(act, gate, a_idx, b_idx, tb):
    n = act.shape[0]
    spec = pl.BlockSpec((tb, PEER_SLOTS), lambda i: (i, 0))
    return pl.pallas_call(
        _peer_w_kernel,
        grid=(n // tb,),
        in_specs=[spec, spec, spec, spec],
        out_specs=pl.BlockSpec((tb, PEER_N_EXPERTS), lambda i: (i, 0)),
        out_shape=jax.ShapeDtypeStruct((n, PEER_N_EXPERTS), BF16),
        scratch_shapes=[pltpu.VMEM((tb, PEER_SLOTS), F32),
                        pltpu.VMEM((tb * W_ROW_PITCH, PEER_N_KEYS), F32)],
        compiler_params=_params(("parallel",)),
        name="peer_expert_weights",
    )(act, gate, a_idx, b_idx)


def _peer_out_kernel(w_ref, v_ref, h1_ref, h1b_ref, p_ref, gw_ref, pw_ref, g2_ref, b2_ref,
                     h2_ref, h2b_ref, acc_ref):
    j = pl.program_id(1)

    @pl.when(j == 0)
    def _():
        acc_ref[...] = jnp.zeros_like(acc_ref)

    acc_ref[...] += _dot(w_ref[...], v_ref[...])

    @pl.when(j == pl.num_programs(1) - 1)
    def _():
        gate = jax.nn.sigmoid(_dot(h1b_ref[...], gw_ref[...]))
        ple = gate * _dot(p_ref[...].astype(BF16), pw_ref[...])
        y = DEEPNORM_ALPHA * h1_ref[...] + acc_ref[...] + ple
        h2 = _layer_norm_rows(y, g2_ref[...], b2_ref[...])
        h2_ref[...] = h2
        h2b_ref[...] = h2.astype(BF16)


def _peer_out_call(w, v, h1, h1b, p, gw, pw, g2, b2, tm, tk):
    n, d = h1.shape
    row = lambda a: a.reshape(1, -1)
    return pl.pallas_call(
        _peer_out_kernel,
        grid=(n // tm, PEER_N_EXPERTS // tk),
        in_specs=[pl.BlockSpec((tm, tk), lambda i, j: (i, j)),
                  pl.BlockSpec((tk, d), lambda i, j: (j, 0)),
                  pl.BlockSpec((tm, d), lambda i, j: (i, 0)),
                  pl.BlockSpec((tm, d), lambda i, j: (i, 0)),
                  pl.BlockSpec((tm, PLE_DIM), lambda i, j: (i, 0)),
                  pl.BlockSpec(gw.shape, lambda i, j: (0, 0)),
                  pl.BlockSpec(pw.shape, lambda i, j: (0, 0)),
                  pl.BlockSpec((1, d), lambda i, j: (0, 0)),
                  pl.BlockSpec((1, d), lambda i, j: (0, 0))],
        out_specs=[pl.BlockSpec((tm, d), lambda i, j: (i, 0)),
                   pl.BlockSpec((tm, d), lambda i, j: (i, 0))],
        out_shape=[jax.ShapeDtypeStruct((n, d), F32), jax.ShapeDtypeStruct((n, d), BF16)],
        scratch_shapes=[pltpu.VMEM((tm, d), F32)],
        compiler_params=_params(("parallel", "arbitrary")),
        name="peer_out_ple_ln2",
    )(w, v, h1, h1b, p, gw, pw, row(g2), row(b2))


def _tiles(n, seq):
    pick = lambda pref, total: pref if total % pref == 0 else total
    return dict(
        rows=pick(512, n),
        tq=pick(256, seq), tk=pick(256, seq),
        retrieve=pick(256, n),
        act_tb=pick(1024, n), act_ec=1024,
        w_tb=pick(128, n),
        out_tm=pick(512, n), out_tk=2048,
    )


def kernel(x, p, emb_ln_g, emb_ln_b, w_in, fox_fb, fox_norm_g, hg_lb_logits, hg_norm_g, w_out,
           ln1_g, ln1_b, peer_wq, peer_keys, peer_u, peer_v, ple_gate_w, ple_w, ln2_g, ln2_b):
    batch, seq, d = x.shape
    n = batch * seq
    tl = _tiles(n, seq)
    h, hb = _ln_call(x.reshape(n, d), emb_ln_g, emb_ln_b, tl["rows"])
    scale = FOX_HEAD_DIM ** -0.5
    o_ff = 3 * FOX_WIDTH
    o_hg = o_ff + FOX_HEADS
    for i in range(DEPTH):
        wi = w_in[i]
        wqkv = jnp.concatenate([wi[:, :FOX_WIDTH] * scale, wi[:, FOX_WIDTH:o_ff]], axis=1).astype(BF16)
        wff = jnp.pad(wi[:, o_ff:o_hg], ((0, 0), (0, LANES - FOX_HEADS))).astype(BF16)
        whg = wi[:, o_hg:].astype(BF16)
        qkv, ff, hgp = _inproj_call(hb, wqkv, wff, whg, tl["rows"])
        ff_t = ff[:, :FOX_HEADS].reshape(batch, seq, FOX_HEADS).transpose(0, 2, 1).reshape(batch * FOX_HEADS, seq)
        fb_rows = jnp.tile(fox_fb[i], batch).reshape(batch * FOX_HEADS, 1)
        c = _fox_cumsum_call(ff_t, fb_rows).reshape(batch, FOX_HEADS // 2, 2, seq)
        fox_o = _fox_call(qkv.reshape(batch, seq, 3 * FOX_WIDTH), c, batch, seq, tl["tq"], tl["tk"])
        lbl = jnp.concatenate([hg_lb_logits.astype(F32), jnp.full((1, HG_WIDTH), float(i), F32)], axis=0)
        ohg = _hgrn_call(hgp.reshape(batch, seq, 4 * HG_WIDTH), lbl, batch, seq)
        h1, h1b = _mix_call(fox_o.reshape(n, FOX_WIDTH), ohg.reshape(n, HG_WIDTH), hgp, h,
                            fox_norm_g[i], hg_norm_g[i], w_out[i].astype(BF16), ln1_g[i], ln1_b[i], tl["rows"])
        keys = peer_keys[i].reshape(PEER_HEADS * 2, PEER_N_KEYS, PEER_HALF_DIM).astype(BF16)
        a_idx, b_idx, gate = _peerq_call(h1b, peer_wq[i].astype(BF16), keys, tl["retrieve"])
        ut = peer_u[i].astype(BF16).T
        act = _peer_act_call(h1b, ut, a_idx, b_idx, tl["act_tb"], tl["act_ec"])
        w = _peer_w_call(act, gate, a_idx, b_idx, tl["w_tb"])
        h, hb = _peer_out_call(w, peer_v[i].astype(BF16), h1, h1b, p[i].reshape(n, PLE_DIM),
                               ple_gate_w[i].astype(BF16), ple_w[i].astype(BF16), ln2_g[i], ln2_b[i],
                               tl["out_tm"], tl["out_tk"])
    return h.reshape(batch, seq, d)
```

```python
import functools

import jax
import jax.numpy as jnp
from jax import lax
from jax.experimental import pallas as pl
from jax.experimental.pallas import tpu as pltpu

D_MODEL = 1024
DEPTH = 2
PLE_DIM = 256
FOX_HEADS = 8
FOX_HEAD_DIM = 64
FOX_WIDTH = FOX_HEADS * FOX_HEAD_DIM
HG_HEADS = 4
HG_KEY_DIM = 128
HG_VAL_DIM = 128
HG_WIDTH = HG_HEADS * HG_VAL_DIM
PEER_HEADS = 8
PEER_N_KEYS = 128
PEER_N_EXPERTS = PEER_N_KEYS * PEER_N_KEYS
PEER_HALF_DIM = 128
PEER_TOPK = 16
PEER_SLOTS = PEER_HEADS * PEER_TOPK
DEEPNORM_ALPHA = (2 * DEPTH) ** 0.25
LN_EPS = 1e-5
RMS_EPS = 1e-6
MASK_VALUE = -1e30
LOG2E = 1.4426950408889634
LB_FLOOR = 1e-30

LANES = 128
SUBLANES = 8
VMEM_LIMIT_BYTES = 56 * 1024 * 1024

HG_CHUNK = 16
HG_CUMSUM_ROWS = 128
W_ROW_PITCH = PEER_N_KEYS + SUBLANES

F32 = jnp.float32
BF16 = jnp.bfloat16


def _params(sem):
    return pltpu.CompilerParams(dimension_semantics=sem, vmem_limit_bytes=VMEM_LIMIT_BYTES)


def _dot(a, b):
    return jnp.dot(a, b, preferred_element_type=F32)


def _dot_nt(a, b):
    return lax.dot_general(a, b, (((1,), (1,)), ((), ())), preferred_element_type=F32)


def _layer_norm_rows(y, g, b):
    mu = jnp.mean(y, axis=-1, keepdims=True)
    d = y - mu
    var = jnp.mean(d * d, axis=-1, keepdims=True)
    return d * lax.rsqrt(var + LN_EPS) * g + b


def _ln_kernel(x_ref, g_ref, b_ref, h_ref, hb_ref):
    h = _layer_norm_rows(x_ref[...], g_ref[...], b_ref[...])
    h_ref[...] = h
    hb_ref[...] = h.astype(BF16)


def _ln_call(x, g, b, tm):
    n, d = x.shape
    return pl.pallas_call(
        _ln_kernel,
        grid=(n // tm,),
        in_specs=[pl.BlockSpec((tm, d), lambda i: (i, 0)),
                  pl.BlockSpec((1, d), lambda i: (0, 0)),
                  pl.BlockSpec((1, d), lambda i: (0, 0))],
        out_specs=[pl.BlockSpec((tm, d), lambda i: (i, 0)),
                   pl.BlockSpec((tm, d), lambda i: (i, 0))],
        out_shape=[jax.ShapeDtypeStruct((n, d), F32), jax.ShapeDtypeStruct((n, d), BF16)],
        compiler_params=_params(("parallel",)),
        name="emb_ln",
    )(x, g.reshape(1, d), b.reshape(1, d))


def _inproj_kernel(h_ref, wqkv_ref, wff_ref, whg_ref, qkv_ref, ff_ref, hg_ref):
    h = h_ref[...]
    qkv_ref[...] = _dot(h, wqkv_ref[...]).astype(BF16)
    ff_ref[...] = _dot(h, wff_ref[...])
    hg_ref[...] = _dot(h, whg_ref[...])


def _inproj_call(hb, wqkv, wff, whg, tm):
    n, d = hb.shape
    return pl.pallas_call(
        _inproj_kernel,
        grid=(n // tm,),
        in_specs=[pl.BlockSpec((tm, d), lambda i: (i, 0)),
                  pl.BlockSpec(wqkv.shape, lambda i: (0, 0)),
                  pl.BlockSpec(wff.shape, lambda i: (0, 0)),
                  pl.BlockSpec(whg.shape, lambda i: (0, 0))],
        out_specs=[pl.BlockSpec((tm, wqkv.shape[1]), lambda i: (i, 0)),
                   pl.BlockSpec((tm, wff.shape[1]), lambda i: (i, 0)),
                   pl.BlockSpec((tm, whg.shape[1]), lambda i: (i, 0))],
        out_shape=[jax.ShapeDtypeStruct((n, wqkv.shape[1]), BF16),
                   jax.ShapeDtypeStruct((n, wff.shape[1]), F32),
                   jax.ShapeDtypeStruct((n, whg.shape[1]), F32)],
        compiler_params=_params(("parallel",)),
        name="in_proj",
    )(hb, wqkv, wff, whg)


def _log_sigmoid(z):
    return jnp.minimum(z, 0.0) - jnp.log(1.0 + jnp.exp(-jnp.abs(z)))


def _fox_cumsum_kernel(ff_ref, fb_ref, c_ref):
    rows, t = ff_ref.shape
    lf = _log_sigmoid(ff_ref[...] + fb_ref[...]) * LOG2E
    r = lax.broadcasted_iota(jnp.int32, (LANES, LANES), 0)
    c = lax.broadcasted_iota(jnp.int32, (LANES, LANES), 1)
    upper = (r <= c).astype(F32)
    carry = jnp.zeros((rows, 1), F32)
    for j in range(t // LANES):
        blk = lf[:, j * LANES:(j + 1) * LANES]
        cs = jnp.dot(blk, upper, preferred_element_type=F32, precision=lax.Precision.HIGHEST)
        cs = cs + carry
        c_ref[:, j * LANES:(j + 1) * LANES] = cs
        carry = cs[:, LANES - 1:LANES]


def _fox_cumsum_call(ff_t, fb_rows):
    rows, t = ff_t.shape
    tr = SUBLANES
    return pl.pallas_call(
        _fox_cumsum_kernel,
        grid=(rows // tr,),
        in_specs=[pl.BlockSpec((tr, t), lambda i: (i, 0)),
                  pl.BlockSpec((tr, 1), lambda i: (i, 0))],
        out_specs=pl.BlockSpec((tr, t), lambda i: (i, 0)),
        out_shape=jax.ShapeDtypeStruct((rows, t), F32),
        compiler_params=_params(("parallel",)),
        name="fox_cumsum",
    )(ff_t, fb_rows)


def _split3(x):
    hi = x.astype(BF16).astype(F32)
    r1 = x - hi
    mid = r1.astype(BF16).astype(F32)
    lo = (r1 - mid).astype(BF16).astype(F32)
    return hi, mid, lo


def _fox_kernel(q_ref, k_ref, v_ref, c_ref, o_ref, kx_ref, vx_ref, ct_ref, *, tq, wide, seq):
    qi = pl.program_id(2)
    lane = lax.broadcasted_iota(jnp.int32, (1, LANES), 1)
    data = [lane < FOX_HEAD_DIM, lane >= FOX_HEAD_DIM]
    spare = [FOX_HEAD_DIM, 0]

    def bias_lanes(h, first3, last3):
        out = jnp.zeros_like(first3[0])
        for j in range(3):
            out = jnp.where(lane == spare[h] + j, first3[j], out)
            out = jnp.where(lane == spare[h] + 3 + j, last3[j], out)
        return out

    @pl.when(qi == 0)
    def _():
        for h in range(2):
            for j in range(seq // LANES):
                rows = slice(j * LANES, (j + 1) * LANES)
                cb = jnp.transpose(jnp.broadcast_to(c_ref[h:h + 1, rows], (LANES, LANES)))
                ct_ref[h, rows, :] = cb
                hi, mid, lo = _split3(cb)
                one = jnp.ones_like(hi)
                extra = bias_lanes(h, (-hi, -mid, -lo), (one, one, one))
                kx_ref[h, rows, :] = jnp.where(data[h], k_ref[rows, :].astype(F32), extra).astype(BF16)
                vx_ref[h, rows, :] = jnp.where(data[h], v_ref[rows, :].astype(F32), one).astype(BF16)

    q0 = pl.multiple_of(qi * tq, tq)
    q = q_ref[...].astype(F32)
    qx = []
    for h in range(2):
        hi, mid, lo = _split3(ct_ref[h, pl.ds(q0, tq), :])
        one = jnp.ones_like(hi)
        qx.append(jnp.where(data[h], q, bias_lanes(h, (one, one, one), (hi, mid, lo))).astype(BF16))

    def block(k0, width, masked, carry):
        out = []
        for h in range(2):
            m_old, acc_old = carry[h]
            s = _dot_nt(qx[h], kx_ref[h, pl.ds(k0, width), :])
            if masked:
                row = lax.broadcasted_iota(jnp.int32, (tq, width), 0) + q0
                col = lax.broadcasted_iota(jnp.int32, (tq, width), 1) + k0
                s = jnp.where(col <= row, s, MASK_VALUE)
            m_new = jnp.maximum(m_old, jnp.max(s, axis=-1, keepdims=True))
            alpha = jnp.exp2(m_old - m_new)
            p = jnp.exp2(s - m_new).astype(BF16)
            acc_new = alpha * acc_old + _dot(p, vx_ref[h, pl.ds(k0, width), :])
            out.append((m_new, acc_new))
        return tuple(out)

    init = tuple((jnp.full((tq, 1), MASK_VALUE, F32), jnp.zeros((tq, LANES), F32)) for _ in range(2))
    n_wide = q0 // wide
    carry = lax.fori_loop(
        0, n_wide, lambda j, c: block(pl.multiple_of(j * wide, wide), wide, False, c), init)
    carry = block(pl.multiple_of(n_wide * wide, wide), wide, True, carry)
    res = []
    for h in range(2):
        acc = carry[h][1]
        denom = pltpu.roll(acc, FOX_HEAD_DIM, axis=1)
        res.append(acc / denom)
    o_ref[...] = jnp.where(data[0], res[0], res[1])


def _fox_call(qkv, c, batch, seq, tq, wide):
    pairs = FOX_HEADS // 2
    return pl.pallas_call(
        functools.partial(_fox_kernel, tq=tq, wide=wide, seq=seq),
        grid=(batch, pairs, seq // tq),
        in_specs=[pl.BlockSpec((None, tq, LANES), lambda b, p, i: (b, i, p)),
                  pl.BlockSpec((None, seq, LANES), lambda b, p, i: (b, 0, pairs + p)),
                  pl.BlockSpec((None, seq, LANES), lambda b, p, i: (b, 0, 2 * pairs + p)),
                  pl.BlockSpec((None, None, 2, seq), lambda b, p, i: (b, p, 0, 0))],
        out_specs=pl.BlockSpec((None, tq, LANES), lambda b, p, i: (b, i, p)),
        out_shape=jax.ShapeDtypeStruct((batch, seq, FOX_WIDTH), F32),
        scratch_shapes=[pltpu.VMEM((2, seq, LANES), BF16),
                        pltpu.VMEM((2, seq, LANES), BF16),
                        pltpu.VMEM((2, seq, LANES), F32)],
        compiler_params=_params(("parallel", "parallel", "arbitrary")),
        name="fox_attention",
    )(qkv, qkv, qkv, c)


def _hgrn_kernel(q_ref, z_ref, v_ref, lbl_ref, o_ref, lf_ref, kk_ref, st_ref, *, seq):
    logits = lbl_ref[...]
    lg = logits[:DEPTH, :]
    mx = jnp.max(lg, axis=0, keepdims=True)
    ex = jnp.exp(lg - mx)
    probs = ex / jnp.sum(ex, axis=0, keepdims=True)
    sel = logits[DEPTH:DEPTH + 1, :]
    lb = jnp.zeros_like(sel)
    run = jnp.zeros_like(sel)
    for i in range(DEPTH):
        run = run + probs[i:i + 1, :]
        lb = jnp.where(sel == float(i), run - probs[0:1, :], lb)
    log_lb = jnp.log(jnp.maximum(lb, LB_FLOOR))
    log_1m = jnp.log(1.0 - lb)

    rr = lax.broadcasted_iota(jnp.int32, (HG_CUMSUM_ROWS, HG_CUMSUM_ROWS), 0)
    cc = lax.broadcasted_iota(jnp.int32, (HG_CUMSUM_ROWS, HG_CUMSUM_ROWS), 1)
    tri = ((rr // HG_CHUNK == cc // HG_CHUNK) & (cc <= rr)).astype(F32)

    def prep(j, _):
        r0 = pl.multiple_of(j * HG_CUMSUM_ROWS, HG_CUMSUM_ROWS)
        z = z_ref[pl.ds(r0, HG_CUMSUM_ROWS), :]
        a = log_1m + _log_sigmoid(z)
        mm = jnp.maximum(log_lb, a)
        lf = mm + jnp.log(1.0 + jnp.exp(-jnp.abs(log_lb - a)))
        lf_ref[pl.ds(r0, HG_CUMSUM_ROWS), :] = jnp.dot(
            tri, lf, preferred_element_type=F32, precision=lax.Precision.HIGHEST)
        kk_ref[pl.ds(r0, HG_CUMSUM_ROWS), :] = (1.0 - lb) * jax.nn.sigmoid(-z)
        return 0

    lax.fori_loop(0, seq // HG_CUMSUM_ROWS, prep, 0)
    st_ref[...] = jnp.zeros_like(st_ref)
    row = lax.broadcasted_iota(jnp.int32, (HG_CHUNK, 1), 0)

    def chunk(ci, _):
        r0 = pl.multiple_of(ci * HG_CHUNK, HG_CHUNK)
        b = lf_ref[pl.ds(r0, HG_CHUNK), :]
        q = q_ref[pl.ds(r0, HG_CHUNK), :]
        kk = kk_ref[pl.ds(r0, HG_CHUNK), :]
        v = v_ref[pl.ds(r0, HG_CHUNK), :]
        b_last = b[HG_CHUNK - 1:HG_CHUNK, :]
        qe = (q * jnp.exp(b)).astype(BF16)
        kd = (kk * jnp.exp(b_last - b)).astype(BF16)
        e_last = jnp.exp(b_last)
        outs = []
        for h in range(HG_HEADS):
            sl = slice(h * HG_KEY_DIM, (h + 1) * HG_KEY_DIM)
            st = st_ref[h]
            o = _dot_nt(qe[:, sl], st.astype(BF16))
            bh, qh, kh, vh = b[:, sl], q[:, sl], kk[:, sl], v[:, sl]
            for s in range(HG_CHUNK):
                d = jnp.exp(jnp.where(row >= s, bh - bh[s:s + 1, :], MASK_VALUE))
                a_col = jnp.sum(qh * kh[s:s + 1, :] * d, axis=-1, keepdims=True)
                o = o + a_col * vh[s:s + 1, :]
            upd = lax.dot_general(vh.astype(BF16), kd[:, sl], (((0,), (0,)), ((), ())),
                                  preferred_element_type=F32)
            st_ref[h] = st * e_last[:, sl] + upd
            outs.append(o)
        o_ref[pl.ds(r0, HG_CHUNK), :] = jnp.concatenate(outs, axis=1)
        return 0

    lax.fori_loop(0, seq // HG_CHUNK, chunk, 0)


def _hgrn_call(hgp, lbl, batch, seq):
    return pl.pallas_call(
        functools.partial(_hgrn_kernel, seq=seq),
        grid=(batch,),
        in_specs=[pl.BlockSpec((None, seq, HG_WIDTH), lambda b: (b, 0, 0)),
                  pl.BlockSpec((None, seq, HG_WIDTH), lambda b: (b, 0, 1)),
                  pl.BlockSpec((None, seq, HG_WIDTH), lambda b: (b, 0, 2)),
                  pl.BlockSpec(lbl.shape, lambda b: (0, 0))],
        out_specs=pl.BlockSpec((None, seq, HG_WIDTH), lambda b: (b, 0, 0)),
        out_shape=jax.ShapeDtypeStruct((batch, seq, HG_WIDTH), F32),
        scratch_shapes=[pltpu.VMEM((seq, HG_WIDTH), F32),
                        pltpu.VMEM((seq, HG_WIDTH), F32),
                        pltpu.VMEM((HG_HEADS, HG_VAL_DIM, HG_KEY_DIM), F32)],
        compiler_params=_params(("parallel",)),
        name="hgrn2",
    )(hgp, hgp, hgp, lbl)


def _mix_kernel(fox_ref, ohg_ref, gate_ref, h_ref, gf_ref, gh_ref, wo_ref, g1_ref, b1_ref,
                h1_ref, h1b_ref):
    fox = fox_ref[...]
    fox = fox * lax.rsqrt(jnp.mean(fox * fox, axis=-1, keepdims=True) + RMS_EPS) * gf_ref[...]
    ohg = ohg_ref[...]
    gh = gh_ref[...]
    gate = gate_ref[...]
    parts = [fox.astype(BF16)]
    for h in range(HG_HEADS):
        sl = slice(h * HG_VAL_DIM, (h + 1) * HG_VAL_DIM)
        o = ohg[:, sl]
        o = o * lax.rsqrt(jnp.mean(o * o, axis=-1, keepdims=True) + RMS_EPS) * gh[:, sl]
        g = gate[:, sl]
        parts.append((o * (g * jax.nn.sigmoid(g))).astype(BF16))
    cat = jnp.concatenate(parts, axis=1)
    mix = _dot(cat, wo_ref[...])
    h1 = _layer_norm_rows(DEEPNORM_ALPHA * h_ref[...] + mix, g1_ref[...], b1_ref[...])
    h1_ref[...] = h1
    h1b_ref[...] = h1.astype(BF16)


def _mix_call(fox_o, ohg, hgp, h, gf, gh, wo, g1, b1, tm):
    n, d = h.shape
    row = lambda a: a.reshape(1, -1)
    return pl.pallas_call(
        _mix_kernel,
        grid=(n // tm,),
        in_specs=[pl.BlockSpec((tm, FOX_WIDTH), lambda i: (i, 0)),
                  pl.BlockSpec((tm, HG_WIDTH), lambda i: (i, 0)),
                  pl.BlockSpec((tm, HG_WIDTH), lambda i: (i, 3)),
                  pl.BlockSpec((tm, d), lambda i: (i, 0)),
                  pl.BlockSpec((1, FOX_WIDTH), lambda i: (0, 0)),
                  pl.BlockSpec((1, HG_WIDTH), lambda i: (0, 0)),
                  pl.BlockSpec(wo.shape, lambda i: (0, 0)),
                  pl.BlockSpec((1, d), lambda i: (0, 0)),
                  pl.BlockSpec((1, d), lambda i: (0, 0))],
        out_specs=[pl.BlockSpec((tm, d), lambda i: (i, 0)),
                   pl.BlockSpec((tm, d), lambda i: (i, 0))],
        out_shape=[jax.ShapeDtypeStruct((n, d), F32), jax.ShapeDtypeStruct((n, d), BF16)],
        compiler_params=_params(("parallel",)),
        name="mix_out_ln1",
    )(fox_o, ohg, hgp, h, row(gf), row(gh), wo, row(g1), row(b1))


def _colmax(x):
    return jnp.max(x, axis=0, keepdims=True)


def _colmin(x):
    return jnp.min(x, axis=0, keepdims=True)


def _top16_rows(s, val_ref, idx_ref):
    n = s.shape[0]
    rows = lax.broadcasted_iota(jnp.int32, s.shape, 0)

    def rnd(r, s):
        m = _colmax(s)
        i = _colmin(jnp.where(s == m, rows, n))
        val_ref[pl.ds(r, 1), :] = m
        idx_ref[pl.ds(r, 1), :] = i
        return jnp.where(rows == i, -jnp.inf, s)

    lax.fori_loop(0, PEER_TOPK, rnd, s)


def _pair_candidates(s1, s2, tb):
    r8 = lax.broadcasted_iota(jnp.int32, (SUBLANES, tb), 0)
    lo2, hi2 = s2[:SUBLANES, :], s2[SUBLANES:, :]
    lo1, hi1 = s1[:SUBLANES, :], s1[SUBLANES:, :]
    ninf = -jnp.inf
    groups = [
        (s1[0:1, :] + lo2, r8),
        (s1[0:1, :] + hi2, r8 + SUBLANES),
        (s1[1:2, :] + lo2, r8 + PEER_TOPK),
        (s1[2:3, :] + lo2, r8 + 2 * PEER_TOPK),
        (s1[3:4, :] + lo2, r8 + 3 * PEER_TOPK),
        (jnp.where(r8 >= 4, lo1 + s2[0:1, :], ninf), r8 * PEER_TOPK),
        (jnp.where(r8 >= 4, lo1 + s2[1:2, :], ninf), r8 * PEER_TOPK + 1),
        (hi1 + s2[0:1, :], (r8 + SUBLANES) * PEER_TOPK),
        (jnp.where(r8 == 4, lo1 + s2[2:3, :], ninf), r8 * PEER_TOPK + 2),
    ]
    return (jnp.concatenate([g[0] for g in groups], axis=0),
            jnp.concatenate([g[1] for g in groups], axis=0))


def _peerq_kernel(h_ref, wq_ref, keys_ref, a_ref, b_ref, g_ref,
                  v1_ref, i1_ref, v2_ref, i2_ref, cv_ref, ci_ref, oa_ref, ob_ref, og_ref):
    tb = h_ref.shape[0]
    q = _dot(h_ref[...], wq_ref[...]).astype(BF16)
    for h in range(PEER_HEADS):
        for p, (vr, ir) in enumerate(((v1_ref, i1_ref), (v2_ref, i2_ref))):
            c0 = (h * 2 + p) * PEER_HALF_DIM
            s = _dot_nt(keys_ref[h * 2 + p], q[:, c0:c0 + PEER_HALF_DIM])
            _top16_rows(s, vr, ir)
        cand, flat = _pair_candidates(v1_ref[...], v2_ref[...], tb)
        big = PEER_TOPK * PEER_TOPK

        def rnd(r, cand):
            m = _colmax(cand)
            f = _colmin(jnp.where(cand == m, flat, big))
            cv_ref[pl.ds(r, 1), :] = m
            ci_ref[pl.ds(r, 1), :] = f
            return jnp.where(flat == f, -jnp.inf, cand)

        lax.fori_loop(0, PEER_TOPK, rnd, cand)
        top_s = cv_ref[...]
        f = ci_ref[...]
        j1 = lax.shift_right_logical(f, 4)
        j2 = f & (PEER_TOPK - 1)
        i1 = i1_ref[...]
        i2 = i2_ref[...]
        a = jnp.zeros_like(f)
        b = jnp.zeros_like(f)
        for j in range(PEER_TOPK):
            a = jnp.where(j1 == j, i1[j:j + 1, :], a)
            b = jnp.where(j2 == j, i2[j:j + 1, :], b)
        e = jnp.exp(top_s - top_s[0:1, :])
        gate = e / jnp.sum(e, axis=0, keepdims=True)
        oa_ref[h * PEER_TOPK:(h + 1) * PEER_TOPK, :] = a
        ob_ref[h * PEER_TOPK:(h + 1) * PEER_TOPK, :] = b
        og_ref[h * PEER_TOPK:(h + 1) * PEER_TOPK, :] = gate
    a_ref[...] = jnp.transpose(oa_ref[...])
    b_ref[...] = jnp.transpose(ob_ref[...])
    g_ref[...] = jnp.transpose(og_ref[...])


def _peerq_call(h1b, wq, keys, tb):
    n, d = h1b.shape
    slot = lambda dt: jax.ShapeDtypeStruct((n, PEER_SLOTS), dt)
    return pl.pallas_call(
        _peerq_kernel,
        grid=(n // tb,),
        in_specs=[pl.BlockSpec((tb, d), lambda i: (i, 0)),
                  pl.BlockSpec(wq.shape, lambda i: (0, 0)),
                  pl.BlockSpec(keys.shape, lambda i: (0, 0, 0))],
        out_specs=[pl.BlockSpec((tb, PEER_SLOTS), lambda i: (i, 0))] * 3,
        out_shape=[slot(jnp.int32), slot(jnp.int32), slot(F32)],
        scratch_shapes=[pltpu.VMEM((PEER_TOPK, tb), F32), pltpu.VMEM((PEER_TOPK, tb), jnp.int32),
                        pltpu.VMEM((PEER_TOPK, tb), F32), pltpu.VMEM((PEER_TOPK, tb), jnp.int32),
                        pltpu.VMEM((PEER_TOPK, tb), F32), pltpu.VMEM((PEER_TOPK, tb), jnp.int32),
                        pltpu.VMEM((PEER_SLOTS, tb), jnp.int32), pltpu.VMEM((PEER_SLOTS, tb), jnp.int32),
                        pltpu.VMEM((PEER_SLOTS, tb), F32)],
        compiler_params=_params(("parallel",)),
        name="peer_retrieve",
    )(h1b, wq, keys)


def _peer_act_kernel(x_ref, ut_ref, a_ref, b_ref, o_ref, *, ec):
    j = pl.program_id(1)

    @pl.when(j == 0)
    def _():
        o_ref[...] = jnp.zeros_like(o_ref)

    act = _dot(x_ref[...], ut_ref[...])
    a_idx = a_ref[...]
    b_idx = b_ref[...]
    acc = o_ref[...]
    groups = ec // PEER_N_KEYS
    for g in range(groups):
        i1 = j * groups + g
        picked = jnp.take_along_axis(act[:, g * PEER_N_KEYS:(g + 1) * PEER_N_KEYS], b_idx, axis=1)
        acc = jnp.where(a_idx == i1, picked, acc)
    o_ref[...] = acc


def _peer_act_call(h1b, ut, a_idx, b_idx, tb, ec):
    n, d = h1b.shape
    return pl.pallas_call(
        functools.partial(_peer_act_kernel, ec=ec),
        grid=(n // tb, PEER_N_EXPERTS // ec),
        in_specs=[pl.BlockSpec((tb, d), lambda i, j: (i, 0)),
                  pl.BlockSpec((d, ec), lambda i, j: (0, j)),
                  pl.BlockSpec((tb, PEER_SLOTS), lambda i, j: (i, 0)),
                  pl.BlockSpec((tb, PEER_SLOTS), lambda i, j: (i, 0))],
        out_specs=pl.BlockSpec((tb, PEER_SLOTS), lambda i, j: (i, 0)),
        out_shape=jax.ShapeDtypeStruct((n, PEER_SLOTS), F32),
        compiler_params=_params(("parallel", "arbitrary")),
        name="peer_expert_in",
    )(h1b, ut, a_idx, b_idx)


def _peer_w_kernel(act_ref, g_ref, a_ref, b_ref, w_ref, wt_ref, w3_ref):
    tb = act_ref.shape[0]
    act = act_ref[...]
    gelu = 0.5 * act * (1.0 + lax.erf(act * (2.0 ** -0.5)))
    wt_ref[...] = g_ref[...] * gelu
    sub = lax.broadcasted_iota(jnp.int32, (PEER_N_KEYS, PEER_SLOTS), 0)

    def token(t, _):
        a_row = a_ref[pl.ds(t, 1), :]
        b_row = b_ref[pl.ds(t, 1), :]
        w_row = wt_ref[pl.ds(t, 1), :]
        pt = jnp.where(sub == a_row, w_row, 0.0).astype(BF16)
        qt = jnp.where(sub == b_row, 1.0, 0.0).astype(BF16)
        r0 = pl.multiple_of(t * W_ROW_PITCH, SUBLANES)
        w3_ref[pl.ds(r0, PEER_N_KEYS), :] = _dot_nt(pt, qt)
        return 0

    lax.fori_loop(0, tb, token, 0, unroll=16)
    for i1 in range(PEER_N_KEYS):
        rows = w3_ref[pl.ds(i1, tb, stride=W_ROW_PITCH), :]
        w_ref[:, i1 * PEER_N_KEYS:(i1 + 1) * PEER_N_KEYS] = rows.astype(BF16)


def _peer_w_call(act, gate, a_idx, b_idx, tb):
    n = act.shape[0]
    spec = pl.BlockSpec((tb, PEER_SLOTS), lambda i: (i, 0))
    return pl.pallas_call(
        _peer_w_kernel,
        grid=(n // tb,),
        in_specs=[spec, spec, spec, spec],
        out_specs=pl.BlockSpec((tb, PEER_N_EXPERTS), lambda i: (i, 0)),
        out_shape=jax.ShapeDtypeStruct((n, PEER_N_EXPERTS), BF16),
        scratch_shapes=[pltpu.VMEM((tb, PEER_SLOTS), F32),
                        pltpu.VMEM((tb * W_ROW_PITCH, PEER_N_KEYS), F32)],
        compiler_params=_params(("parallel",)),
        name="peer_expert_weights",
    )(act, gate, a_idx, b_idx)


def _peer_out_kernel(w_ref, v_ref, h1_ref, h1b_ref, p_ref, gw_ref, pw_ref, g2_ref, b2_ref,
                     h2_ref, h2b_ref, acc_ref):
    j = pl.program_id(1)

    @pl.when(j == 0)
    def _():
        acc_ref[...] = jnp.zeros_like(acc_ref)

    acc_ref[...] += _dot(w_ref[...], v_ref[...])

    @pl.when(j == pl.num_programs(1) - 1)
    def _():
        gate = jax.nn.sigmoid(_dot(h1b_ref[...], gw_ref[...]))
        ple = gate * _dot(p_ref[...].astype(BF16), pw_ref[...])
        y = DEEPNORM_ALPHA * h1_ref[...] + acc_ref[...] + ple
        h2 = _layer_norm_rows(y, g2_ref[...], b2_ref[...])
        h2_ref[...] = h2
        h2b_ref[...] = h2.astype(BF16)


def _peer_out_call(w, v, h1, h1b, p, gw, pw, g2, b2, tm, tk):
    n, d = h1.shape
    row = lambda a: a.reshape(1, -1)
    return pl.pallas_call(
        _peer_out_kernel,
        grid=(n // tm, PEER_N_EXPERTS // tk),
        in_specs=[pl.BlockSpec((tm, tk), lambda i, j: (i, j)),
                  pl.BlockSpec((tk, d), lambda i, j: (j, 0)),
                  pl.BlockSpec((tm, d), lambda i, j: (i, 0)),
                  pl.BlockSpec((tm, d), lambda i, j: (i, 0)),
                  pl.BlockSpec((tm, PLE_DIM), lambda i, j: (i, 0)),
                  pl.BlockSpec(gw.shape, lambda i, j: (0, 0)),
                  pl.BlockSpec(pw.shape, lambda i, j: (0, 0)),
                  pl.BlockSpec((1, d), lambda i, j: (0, 0)),
                  pl.BlockSpec((1, d), lambda i, j: (0, 0))],
        out_specs=[pl.BlockSpec((tm, d), lambda i, j: (i, 0)),
                   pl.BlockSpec((tm, d), lambda i, j: (i, 0))],
        out_shape=[jax.ShapeDtypeStruct((n, d), F32), jax.ShapeDtypeStruct((n, d), BF16)],
        scratch_shapes=[pltpu.VMEM((tm, d), F32)],
        compiler_params=_params(("parallel", "arbitrary")),
        name="peer_out_ple_ln2",
    )(w, v, h1, h1b, p, gw, pw, row(g2), row(b2))


def _tiles(n, seq):
    pick = lambda pref, total: pref if total % pref == 0 else total
    return dict(
        rows=pick(512, n),
        tq=pick(512, seq), wide=pick(1024, seq),
        retrieve=pick(256, n),
        act_tb=pick(1024, n), act_ec=2048,
        w_tb=pick(128, n),
        out_tm=pick(512, n), out_tk=4096,
    )


def kernel(x, p, emb_ln_g, emb_ln_b, w_in, fox_fb, fox_norm_g, hg_lb_logits, hg_norm_g, w_out,
           ln1_g, ln1_b, peer_wq, peer_keys, peer_u, peer_v, ple_gate_w, ple_w, ln2_g, ln2_b):
    batch, seq, d = x.shape
    n = batch * seq
    tl = _tiles(n, seq)
    h, hb = _ln_call(x.reshape(n, d), emb_ln_g, emb_ln_b, tl["rows"])
    scale = FOX_HEAD_DIM ** -0.5 * LOG2E
    o_ff = 3 * FOX_WIDTH
    o_hg = o_ff + FOX_HEADS
    for i in range(DEPTH):
        wi = w_in[i]
        wqkv = jnp.concatenate([wi[:, :FOX_WIDTH] * scale, wi[:, FOX_WIDTH:o_ff]], axis=1).astype(BF16)
        wff = jnp.pad(wi[:, o_ff:o_hg], ((0, 0), (0, LANES - FOX_HEADS))).astype(BF16)
        whg = wi[:, o_hg:].astype(BF16)
        qkv, ff, hgp = _inproj_call(hb, wqkv, wff, whg, tl["rows"])
        ff_t = ff[:, :FOX_HEADS].reshape(batch, seq, FOX_HEADS).transpose(0, 2, 1).reshape(batch * FOX_HEADS, seq)
        fb_rows = jnp.tile(fox_fb[i], batch).reshape(batch * FOX_HEADS, 1)
        c = _fox_cumsum_call(ff_t, fb_rows).reshape(batch, FOX_HEADS // 2, 2, seq)
        fox_o = _fox_call(qkv.reshape(batch, seq, 3 * FOX_WIDTH), c, batch, seq, tl["tq"], tl["wide"])
        lbl = jnp.concatenate([hg_lb_logits.astype(F32), jnp.full((1, HG_WIDTH), float(i), F32)], axis=0)
        ohg = _hgrn_call(hgp.reshape(batch, seq, 4 * HG_WIDTH), lbl, batch, seq)
        h1, h1b = _mix_call(fox_o.reshape(n, FOX_WIDTH), ohg.reshape(n, HG_WIDTH), hgp, h,
                            fox_norm_g[i], hg_norm_g[i], w_out[i].astype(BF16), ln1_g[i], ln1_b[i], tl["rows"])
        keys = peer_keys[i].reshape(PEER_HEADS * 2, PEER_N_KEYS, PEER_HALF_DIM).astype(BF16)
        a_idx, b_idx, gate = _peerq_call(h1b, peer_wq[i].astype(BF16), keys, tl["retrieve"])
        ut = peer_u[i].astype(BF16).T
        act = _peer_act_call(h1b, ut, a_idx, b_idx, tl["act_tb"], tl["act_ec"])
        w = _peer_w_call(act, gate, a_idx, b_idx, tl["w_tb"])
        h, hb = _peer_out_call(w, peer_v[i].astype(BF16), h1, h1b, p[i].reshape(n, PLE_DIM),
                               ple_gate_w[i].astype(BF16), ple_w[i].astype(BF16), ln2_g[i], ln2_b[i],
                               tl["out_tm"], tl["out_tk"])
    return h.reshape(batch, seq, d)
```

```python
import functools

import jax
import jax.numpy as jnp
from jax import lax
from jax.experimental import pallas as pl
from jax.experimental.pallas import tpu as pltpu

D_MODEL = 1024
DEPTH = 2
PLE_DIM = 256
FOX_HEADS = 8
FOX_HEAD_DIM = 64
FOX_WIDTH = FOX_HEADS * FOX_HEAD_DIM
HG_HEADS = 4
HG_KEY_DIM = 128
HG_VAL_DIM = 128
HG_WIDTH = HG_HEADS * HG_VAL_DIM
PEER_HEADS = 8
PEER_N_KEYS = 128
PEER_N_EXPERTS = PEER_N_KEYS * PEER_N_KEYS
PEER_HALF_DIM = 128
PEER_TOPK = 16
PEER_SLOTS = PEER_HEADS * PEER_TOPK
DEEPNORM_ALPHA = (2 * DEPTH) ** 0.25
LN_EPS = 1e-5
RMS_EPS = 1e-6
MASK_VALUE = -1e30
LOG2E = 1.4426950408889634
LB_FLOOR = 1e-30

LANES = 128
SUBLANES = 8
VMEM_LIMIT_BYTES = 56 * 1024 * 1024

HG_CHUNK = 16
HG_CUMSUM_ROWS = 128
W_ROW_PITCH = PEER_N_KEYS + SUBLANES

F32 = jnp.float32
BF16 = jnp.bfloat16


def _params(sem):
    return pltpu.CompilerParams(dimension_semantics=sem, vmem_limit_bytes=VMEM_LIMIT_BYTES)


def _dot(a, b):
    return jnp.dot(a, b, preferred_element_type=F32)


def _dot_nt(a, b):
    return lax.dot_general(a, b, (((1,), (1,)), ((), ())), preferred_element_type=F32)


def _layer_norm_rows(y, g, b):
    mu = jnp.mean(y, axis=-1, keepdims=True)
    d = y - mu
    var = jnp.mean(d * d, axis=-1, keepdims=True)
    return d * lax.rsqrt(var + LN_EPS) * g + b


def _ln_kernel(x_ref, g_ref, b_ref, h_ref, hb_ref):
    h = _layer_norm_rows(x_ref[...], g_ref[...], b_ref[...])
    h_ref[...] = h
    hb_ref[...] = h.astype(BF16)


def _ln_call(x, g, b, tm):
    n, d = x.shape
    return pl.pallas_call(
        _ln_kernel,
        grid=(n // tm,),
        in_specs=[pl.BlockSpec((tm, d), lambda i: (i, 0)),
                  pl.BlockSpec((1, d), lambda i: (0, 0)),
                  pl.BlockSpec((1, d), lambda i: (0, 0))],
        out_specs=[pl.BlockSpec((tm, d), lambda i: (i, 0)),
                   pl.BlockSpec((tm, d), lambda i: (i, 0))],
        out_shape=[jax.ShapeDtypeStruct((n, d), F32), jax.ShapeDtypeStruct((n, d), BF16)],
        compiler_params=_params(("parallel",)),
        name="emb_ln",
    )(x, g.reshape(1, d), b.reshape(1, d))


def _inproj_kernel(h_ref, wqkv_ref, wff_ref, whg_ref, qkv_ref, ff_ref, hg_ref):
    h = h_ref[...]
    qkv_ref[...] = _dot(h, wqkv_ref[...]).astype(BF16)
    ff_ref[...] = _dot(h, wff_ref[...])
    hg_ref[...] = _dot(h, whg_ref[...])


def _inproj_call(hb, wqkv, wff, whg, tm):
    n, d = hb.shape
    return pl.pallas_call(
        _inproj_kernel,
        grid=(n // tm,),
        in_specs=[pl.BlockSpec((tm, d), lambda i: (i, 0)),
                  pl.BlockSpec(wqkv.shape, lambda i: (0, 0)),
                  pl.BlockSpec(wff.shape, lambda i: (0, 0)),
                  pl.BlockSpec(whg.shape, lambda i: (0, 0))],
        out_specs=[pl.BlockSpec((tm, wqkv.shape[1]), lambda i: (i, 0)),
                   pl.BlockSpec((tm, wff.shape[1]), lambda i: (i, 0)),
                   pl.BlockSpec((tm, whg.shape[1]), lambda i: (i, 0))],
        out_shape=[jax.ShapeDtypeStruct((n, wqkv.shape[1]), BF16),
                   jax.ShapeDtypeStruct((n, wff.shape[1]), F32),
                   jax.ShapeDtypeStruct((n, whg.shape[1]), F32)],
        compiler_params=_params(("parallel",)),
        name="in_proj",
    )(hb, wqkv, wff, whg)


def _log_sigmoid(z):
    return jnp.minimum(z, 0.0) - jnp.log(1.0 + jnp.exp(-jnp.abs(z)))


def _fox_cumsum_kernel(ff_ref, fb_ref, c_ref):
    rows, t = ff_ref.shape
    lf = _log_sigmoid(ff_ref[...] + fb_ref[...]) * LOG2E
    r = lax.broadcasted_iota(jnp.int32, (LANES, LANES), 0)
    c = lax.broadcasted_iota(jnp.int32, (LANES, LANES), 1)
    upper = (r <= c).astype(F32)
    carry = jnp.zeros((rows, 1), F32)
    for j in range(t // LANES):
        blk = lf[:, j * LANES:(j + 1) * LANES]
        cs = jnp.dot(blk, upper, preferred_element_type=F32, precision=lax.Precision.HIGHEST)
        cs = cs + carry
        c_ref[:, j * LANES:(j + 1) * LANES] = cs
        carry = cs[:, LANES - 1:LANES]


def _fox_cumsum_call(ff_t, fb_rows):
    rows, t = ff_t.shape
    tr = SUBLANES
    return pl.pallas_call(
        _fox_cumsum_kernel,
        grid=(rows // tr,),
        in_specs=[pl.BlockSpec((tr, t), lambda i: (i, 0)),
                  pl.BlockSpec((tr, 1), lambda i: (i, 0))],
        out_specs=pl.BlockSpec((tr, t), lambda i: (i, 0)),
        out_shape=jax.ShapeDtypeStruct((rows, t), F32),
        compiler_params=_params(("parallel",)),
        name="fox_cumsum",
    )(ff_t, fb_rows)


def _split3(x):
    hi = x.astype(BF16).astype(F32)
    r1 = x - hi
    mid = r1.astype(BF16).astype(F32)
    lo = (r1 - mid).astype(BF16).astype(F32)
    return hi, mid, lo


def _fox_kernel(q_ref, k_ref, v_ref, c_ref, o_ref, kx_ref, vx_ref, ct_ref, *, tq, wide, seq):
    qi = pl.program_id(2)
    lane = lax.broadcasted_iota(jnp.int32, (1, LANES), 1)
    data = [lane < FOX_HEAD_DIM, lane >= FOX_HEAD_DIM]
    spare = [FOX_HEAD_DIM, 0]

    def bias_lanes(h, first3, last3):
        out = jnp.zeros_like(first3[0])
        for j in range(3):
            out = jnp.where(lane == spare[h] + j, first3[j], out)
            out = jnp.where(lane == spare[h] + 3 + j, last3[j], out)
        return out

    @pl.when(qi == 0)
    def _():
        for h in range(2):
            for j in range(seq // LANES):
                rows = slice(j * LANES, (j + 1) * LANES)
                cb = jnp.transpose(jnp.broadcast_to(c_ref[h:h + 1, rows], (LANES, LANES)))
                ct_ref[h, rows, :] = cb
                hi, mid, lo = _split3(cb)
                one = jnp.ones_like(hi)
                extra = bias_lanes(h, (-hi, -mid, -lo), (one, one, one))
                kx_ref[h, rows, :] = jnp.where(data[h], k_ref[rows, :].astype(F32), extra).astype(BF16)
                vx_ref[h, rows, :] = jnp.where(data[h], v_ref[rows, :].astype(F32), one).astype(BF16)

    q0 = pl.multiple_of(qi * tq, tq)
    q = q_ref[...].astype(F32)
    qx = []
    for h in range(2):
        hi, mid, lo = _split3(ct_ref[h, pl.ds(q0, tq), :])
        one = jnp.ones_like(hi)
        qx.append(jnp.where(data[h], q, bias_lanes(h, (one, one, one), (hi, mid, lo))).astype(BF16))

    def block(k0, width, masked, carry):
        out = []
        for h in range(2):
            m_old, acc_old = carry[h]
            s = _dot_nt(qx[h], kx_ref[h, pl.ds(k0, width), :])
            if masked:
                row = lax.broadcasted_iota(jnp.int32, (tq, width), 0) + q0
                col = lax.broadcasted_iota(jnp.int32, (tq, width), 1) + k0
                s = jnp.where(col <= row, s, MASK_VALUE)
            m_new = jnp.maximum(m_old, jnp.max(s, axis=-1, keepdims=True))
            alpha = jnp.exp2(m_old - m_new)
            p = jnp.exp2(s - m_new).astype(BF16)
            acc_new = alpha * acc_old + _dot(p, vx_ref[h, pl.ds(k0, width), :])
            out.append((m_new, acc_new))
        return tuple(out)

    init = tuple((jnp.full((tq, 1), MASK_VALUE, F32), jnp.zeros((tq, LANES), F32)) for _ in range(2))
    n_wide = q0 // wide
    carry = lax.fori_loop(
        0, n_wide, lambda j, c: block(pl.multiple_of(j * wide, wide), wide, False, c), init)
    carry = block(pl.multiple_of(n_wide * wide, wide), wide, True, carry)
    res = []
    for h in range(2):
        acc = carry[h][1]
        denom = pltpu.roll(acc, FOX_HEAD_DIM, axis=1)
        res.append(acc / denom)
    o_ref[...] = jnp.where(data[0], res[0], res[1])


def _fox_call(qkv, c, batch, seq, tq, wide):
    pairs = FOX_HEADS // 2
    return pl.pallas_call(
        functools.partial(_fox_kernel, tq=tq, wide=wide, seq=seq),
        grid=(batch, pairs, seq // tq),
        in_specs=[pl.BlockSpec((None, tq, LANES), lambda b, p, i: (b, i, p)),
                  pl.BlockSpec((None, seq, LANES), lambda b, p, i: (b, 0, pairs + p)),
                  pl.BlockSpec((None, seq, LANES), lambda b, p, i: (b, 0, 2 * pairs + p)),
                  pl.BlockSpec((None, None, 2, seq), lambda b, p, i: (b, p, 0, 0))],
        out_specs=pl.BlockSpec((None, tq, LANES), lambda b, p, i: (b, i, p)),
        out_shape=jax.ShapeDtypeStruct((batch, seq, FOX_WIDTH), F32),
        scratch_shapes=[pltpu.VMEM((2, seq, LANES), BF16),
                        pltpu.VMEM((2, seq, LANES), BF16),
                        pltpu.VMEM((2, seq, LANES), F32)],
        compiler_params=_params(("parallel", "parallel", "arbitrary")),
        name="fox_attention",
    )(qkv, qkv, qkv, c)


def _hgrn_kernel(q_ref, z_ref, v_ref, lbl_ref, o_ref, lf_ref, kk_ref, st_ref, *, seq):
    logits = lbl_ref[...]
    lg = logits[:DEPTH, :]
    mx = jnp.max(lg, axis=0, keepdims=True)
    ex = jnp.exp(lg - mx)
    probs = ex / jnp.sum(ex, axis=0, keepdims=True)
    sel = logits[DEPTH:DEPTH + 1, :]
    lb = jnp.zeros_like(sel)
    run = jnp.zeros_like(sel)
    for i in range(DEPTH):
        run = run + probs[i:i + 1, :]
        lb = jnp.where(sel == float(i), run - probs[0:1, :], lb)
    log_lb = jnp.log(jnp.maximum(lb, LB_FLOOR))
    log_1m = jnp.log(1.0 - lb)

    rr = lax.broadcasted_iota(jnp.int32, (HG_CUMSUM_ROWS, HG_CUMSUM_ROWS), 0)
    cc = lax.broadcasted_iota(jnp.int32, (HG_CUMSUM_ROWS, HG_CUMSUM_ROWS), 1)
    tri = ((rr // HG_CHUNK == cc // HG_CHUNK) & (cc <= rr)).astype(F32)

    def prep(j, _):
        r0 = pl.multiple_of(j * HG_CUMSUM_ROWS, HG_CUMSUM_ROWS)
        z = z_ref[pl.ds(r0, HG_CUMSUM_ROWS), :]
        a = log_1m + _log_sigmoid(z)
        mm = jnp.maximum(log_lb, a)
        lf = mm + jnp.log(1.0 + jnp.exp(-jnp.abs(log_lb - a)))
        lf_ref[pl.ds(r0, HG_CUMSUM_ROWS), :] = jnp.dot(
            tri, lf, preferred_element_type=F32, precision=lax.Precision.HIGHEST)
        kk_ref[pl.ds(r0, HG_CUMSUM_ROWS), :] = (1.0 - lb) * jax.nn.sigmoid(-z)
        return 0

    lax.fori_loop(0, seq // HG_CUMSUM_ROWS, prep, 0)
    st_ref[...] = jnp.zeros_like(st_ref)
    row = lax.broadcasted_iota(jnp.int32, (HG_CHUNK, 1), 0)

    def chunk(ci, _):
        r0 = pl.multiple_of(ci * HG_CHUNK, HG_CHUNK)
        b = lf_ref[pl.ds(r0, HG_CHUNK), :]
        q = q_ref[pl.ds(r0, HG_CHUNK), :]
        kk = kk_ref[pl.ds(r0, HG_CHUNK), :]
        v = v_ref[pl.ds(r0, HG_CHUNK), :]
        b_last = b[HG_CHUNK - 1:HG_CHUNK, :]
        qe = (q * jnp.exp(b)).astype(BF16)
        kd = (kk * jnp.exp(b_last - b)).astype(BF16)
        e_last = jnp.exp(b_last)
        outs = []
        for h in range(HG_HEADS):
            sl = slice(h * HG_KEY_DIM, (h + 1) * HG_KEY_DIM)
            st = st_ref[h]
            o = _dot_nt(qe[:, sl], st.astype(BF16))
            bh, qh, kh, vh = b[:, sl], q[:, sl], kk[:, sl], v[:, sl]
            for s in range(HG_CHUNK):
                d = jnp.exp(jnp.where(row >= s, bh - bh[s:s + 1, :], MASK_VALUE))
                a_col = jnp.sum(qh * kh[s:s + 1, :] * d, axis=-1, keepdims=True)
                o = o + a_col * vh[s:s + 1, :]
            upd = lax.dot_general(vh.astype(BF16), kd[:, sl], (((0,), (0,)), ((), ())),
                                  preferred_element_type=F32)
            st_ref[h] = st * e_last[:, sl] + upd
            outs.append(o)
        o_ref[pl.ds(r0, HG_CHUNK), :] = jnp.concatenate(outs, axis=1)
        return 0

    lax.fori_loop(0, seq // HG_CHUNK, chunk, 0)


def _hgrn_call(hgp, lbl, batch, seq):
    return pl.pallas_call(
        functools.partial(_hgrn_kernel, seq=seq),
        grid=(batch,),
        in_specs=[pl.BlockSpec((None, seq, HG_WIDTH), lambda b: (b, 0, 0)),
                  pl.BlockSpec((None, seq, HG_WIDTH), lambda b: (b, 0, 1)),
                  pl.BlockSpec((None, seq, HG_WIDTH), lambda b: (b, 0, 2)),
                  pl.BlockSpec(lbl.shape, lambda b: (0, 0))],
        out_specs=pl.BlockSpec((None, seq, HG_WIDTH), lambda b: (b, 0, 0)),
        out_shape=jax.ShapeDtypeStruct((batch, seq, HG_WIDTH), F32),
        scratch_shapes=[pltpu.VMEM((seq, HG_WIDTH), F32),
                        pltpu.VMEM((seq, HG_WIDTH), F32),
                        pltpu.VMEM((HG_HEADS, HG_VAL_DIM, HG_KEY_DIM), F32)],
        compiler_params=_params(("parallel",)),
        name="hgrn2",
    )(hgp, hgp, hgp, lbl)


def _mix_kernel(fox_ref, ohg_ref, gate_ref, h_ref, gf_ref, gh_ref, wo_ref, g1_ref, b1_ref,
                h1_ref, h1b_ref):
    fox = fox_ref[...]
    fox = fox * lax.rsqrt(jnp.mean(fox * fox, axis=-1, keepdims=True) + RMS_EPS) * gf_ref[...]
    ohg = ohg_ref[...]
    gh = gh_ref[...]
    gate = gate_ref[...]
    parts = [fox.astype(BF16)]
    for h in range(HG_HEADS):
        sl = slice(h * HG_VAL_DIM, (h + 1) * HG_VAL_DIM)
        o = ohg[:, sl]
        o = o * lax.rsqrt(jnp.mean(o * o, axis=-1, keepdims=True) + RMS_EPS) * gh[:, sl]
        g = gate[:, sl]
        parts.append((o * (g * jax.nn.sigmoid(g))).astype(BF16))
    cat = jnp.concatenate(parts, axis=1)
    mix = _dot(cat, wo_ref[...])
    h1 = _layer_norm_rows(DEEPNORM_ALPHA * h_ref[...] + mix, g1_ref[...], b1_ref[...])
    h1_ref[...] = h1
    h1b_ref[...] = h1.astype(BF16)


def _mix_call(fox_o, ohg, hgp, h, gf, gh, wo, g1, b1, tm):
    n, d = h.shape
    row = lambda a: a.reshape(1, -1)
    return pl.pallas_call(
        _mix_kernel,
        grid=(n // tm,),
        in_specs=[pl.BlockSpec((tm, FOX_WIDTH), lambda i: (i, 0)),
                  pl.BlockSpec((tm, HG_WIDTH), lambda i: (i, 0)),
                  pl.BlockSpec((tm, HG_WIDTH), lambda i: (i, 3)),
                  pl.BlockSpec((tm, d), lambda i: (i, 0)),
                  pl.BlockSpec((1, FOX_WIDTH), lambda i: (0, 0)),
                  pl.BlockSpec((1, HG_WIDTH), lambda i: (0, 0)),
                  pl.BlockSpec(wo.shape, lambda i: (0, 0)),
                  pl.BlockSpec((1, d), lambda i: (0, 0)),
                  pl.BlockSpec((1, d), lambda i: (0, 0))],
        out_specs=[pl.BlockSpec((tm, d), lambda i: (i, 0)),
                   pl.BlockSpec((tm, d), lambda i: (i, 0))],
        out_shape=[jax.ShapeDtypeStruct((n, d), F32), jax.ShapeDtypeStruct((n, d), BF16)],
        compiler_params=_params(("parallel",)),
        name="mix_out_ln1",
    )(fox_o, ohg, hgp, h, row(gf), row(gh), wo, row(g1), row(b1))


class _Network:
    def __init__(self):
        self.comparators = []
        self.outputs = []


def _net_merge(net, xs, ys):
    if not xs:
        return list(ys)
    if not ys:
        return list(xs)
    if len(xs) == 1 and len(ys) == 1:
        net.comparators.append((xs[0], ys[0]))
        return [xs[0], ys[0]]
    evens = _net_merge(net, xs[0::2], ys[0::2])
    odds = _net_merge(net, xs[1::2], ys[1::2])
    out = [evens[0]]
    for i, w in enumerate(odds):
        if i + 1 < len(evens):
            net.comparators.append((evens[i + 1], w))
            out += [evens[i + 1], w]
        else:
            out.append(w)
    return out + evens[len(odds) + 1:]


def _net_sort(net, ws):
    if len(ws) <= 1:
        return list(ws)
    mid = len(ws) // 2
    return _net_merge(net, _net_sort(net, ws[:mid]), _net_sort(net, ws[mid:]))


def _top_network(sizes, presorted, keep):
    net = _Network()
    lists, w = [], 0
    for s in sizes:
        ws = list(range(w, w + s))
        w += s
        lists.append(ws if presorted else _net_sort(net, ws))
    while len(lists) > 1:
        lists.sort(key=len)
        lists = [_net_merge(net, lists[0], lists[1])[:keep]] + lists[2:]
    net.outputs = lists[0][:keep]
    needed, kept = set(net.outputs), []
    for a, b in reversed(net.comparators):
        if a in needed or b in needed:
            kept.append((a, b, a in needed, b in needed))
            needed.update((a, b))
    net.comparators = kept[::-1]
    return net


def _run_network(net, wires):
    wires = list(wires)
    for a, b, need_a, need_b in net.comparators:
        (va, ia), (vb, ib) = wires[a], wires[b]
        a_first = (va > vb) | ((va == vb) & (ia < ib))
        if need_a:
            wires[a] = (jnp.where(a_first, va, vb), jnp.where(a_first, ia, ib))
        if need_b:
            wires[b] = (jnp.where(a_first, vb, va), jnp.where(a_first, ib, ia))
    return [wires[w] for w in net.outputs]


def _pair_lists():
    ok = lambda j1, j2: (j1 + 1) * (j2 + 1) <= PEER_TOPK
    lists = []
    for d in range(PEER_TOPK):
        row = [(d, j2) for j2 in range(d, PEER_TOPK) if ok(d, j2)]
        col = [(j1, d) for j1 in range(d + 1, PEER_TOPK) if ok(j1, d)]
        lists += [l for l in (row, col) if l]
    return lists


FLAT_SHIFT = 14
_KEY_NET = _top_network([PEER_TOPK] * (PEER_N_KEYS // PEER_TOPK), presorted=False, keep=PEER_TOPK)
_PAIR_LISTS = _pair_lists()
_PAIR_NET = _top_network([len(l) for l in _PAIR_LISTS], presorted=True, keep=PEER_TOPK)


SLAB_TOKENS = SUBLANES * LANES
SLAB_ROWS = SUBLANES * W_ROW_PITCH


def _peerq_kernel(h_ref, wq_ref, keys_ref, a_ref, b_ref, g_ref,
                  sc_ref, tv_ref, ti_ref, oa_ref, ob_ref, og_ref):
    n_lists = 2 * PEER_HEADS
    q = _dot(h_ref[...], wq_ref[...]).astype(BF16)
    for hp in range(n_lists):
        s = _dot_nt(keys_ref[hp], q[:, hp * PEER_HALF_DIM:(hp + 1) * PEER_HALF_DIM])
        for g in range(SUBLANES):
            sc_ref[pl.ds(hp * SLAB_ROWS + g * W_ROW_PITCH, PEER_N_KEYS), :] = s[:, g * LANES:(g + 1) * LANES]

    def keys_top(hp, _):
        base = hp * SLAB_ROWS
        wires = [(sc_ref[pl.ds(base + n, SUBLANES, stride=W_ROW_PITCH), :],
                  jnp.full((SUBLANES, LANES), n, jnp.int32)) for n in range(PEER_N_KEYS)]
        for j, (v, i) in enumerate(_run_network(_KEY_NET, wires)):
            tv_ref[hp * PEER_TOPK + j] = v
            ti_ref[hp * PEER_TOPK + j] = i
        return 0

    lax.fori_loop(0, n_lists, keys_top, 0)

    def pairs_top(h, _):
        r1 = 2 * h * PEER_TOPK
        r2 = r1 + PEER_TOPK
        s1 = [tv_ref[r1 + j] for j in range(PEER_TOPK)]
        s2 = [tv_ref[r2 + j] for j in range(PEER_TOPK)]
        e1 = [ti_ref[r1 + j] * PEER_N_KEYS for j in range(PEER_TOPK)]
        i2 = [ti_ref[r2 + j] for j in range(PEER_TOPK)]
        wires = [(s1[j1] + s2[j2], e1[j1] + i2[j2] + ((j1 * PEER_TOPK + j2) << FLAT_SHIFT))
                 for lst in _PAIR_LISTS for (j1, j2) in lst]
        top = _run_network(_PAIR_NET, wires)
        es = [jnp.exp(v - top[0][0]) for v, _ in top]
        total = es[0]
        for e in es[1:]:
            total = total + e
        inv = 1.0 / total
        for k, ((_, key), e) in enumerate(zip(top, es)):
            expert = key & ((1 << FLAT_SHIFT) - 1)
            rows = pl.ds(h * PEER_TOPK + k, SUBLANES, stride=W_ROW_PITCH)
            oa_ref[rows, :] = lax.shift_right_logical(expert, 7)
            ob_ref[rows, :] = expert & (PEER_N_KEYS - 1)
            og_ref[rows, :] = e * inv
        return 0

    lax.fori_loop(0, PEER_HEADS, pairs_top, 0)
    for g in range(SUBLANES):
        rows = pl.ds(g * W_ROW_PITCH, PEER_SLOTS)
        toks = slice(g * LANES, (g + 1) * LANES)
        a_ref[toks, :] = jnp.transpose(oa_ref[rows, :])
        b_ref[toks, :] = jnp.transpose(ob_ref[rows, :])
        g_ref[toks, :] = jnp.transpose(og_ref[rows, :])


def _peerq_call(h1b, wq, keys):
    n, d = h1b.shape
    tb = SLAB_TOKENS
    slot = lambda dt: jax.ShapeDtypeStruct((n, PEER_SLOTS), dt)
    return pl.pallas_call(
        _peerq_kernel,
        grid=(n // tb,),
        in_specs=[pl.BlockSpec((tb, d), lambda i: (i, 0)),
                  pl.BlockSpec(wq.shape, lambda i: (0, 0)),
                  pl.BlockSpec(keys.shape, lambda i: (0, 0, 0))],
        out_specs=[pl.BlockSpec((tb, PEER_SLOTS), lambda i: (i, 0))] * 3,
        out_shape=[slot(jnp.int32), slot(jnp.int32), slot(F32)],
        scratch_shapes=[pltpu.VMEM((2 * PEER_HEADS * SLAB_ROWS, LANES), F32),
                        pltpu.VMEM((2 * PEER_HEADS * PEER_TOPK, SUBLANES, LANES), F32),
                        pltpu.VMEM((2 * PEER_HEADS * PEER_TOPK, SUBLANES, LANES), jnp.int32),
                        pltpu.VMEM((SLAB_ROWS, LANES), jnp.int32),
                        pltpu.VMEM((SLAB_ROWS, LANES), jnp.int32),
                        pltpu.VMEM((SLAB_ROWS, LANES), F32)],
        compiler_params=_params(("parallel",)),
        name="peer_retrieve",
    )(h1b, wq, keys)


def _peer_act_kernel(x_ref, ut_ref, a_ref, b_ref, o_ref, *, ec):
    j = pl.program_id(1)

    @pl.when(j == 0)
    def _():
        o_ref[...] = jnp.zeros_like(o_ref)

    act = _dot(x_ref[...], ut_ref[...])
    a_idx = a_ref[...]
    b_idx = b_ref[...]
    acc = o_ref[...]
    groups = ec // PEER_N_KEYS
    for g in range(groups):
        i1 = j * groups + g
        picked = jnp.take_along_axis(act[:, g * PEER_N_KEYS:(g + 1) * PEER_N_KEYS], b_idx, axis=1)
        acc = jnp.where(a_idx == i1, picked, acc)
    o_ref[...] = acc


def _peer_act_call(h1b, ut, a_idx, b_idx, tb, ec):
    n, d = h1b.shape
    return pl.pallas_call(
        functools.partial(_peer_act_kernel, ec=ec),
        grid=(n // tb, PEER_N_EXPERTS // ec),
        in_specs=[pl.BlockSpec((tb, d), lambda i, j: (i, 0)),
                  pl.BlockSpec((d, ec), lambda i, j: (0, j)),
                  pl.BlockSpec((tb, PEER_SLOTS), lambda i, j: (i, 0)),
                  pl.BlockSpec((tb, PEER_SLOTS), lambda i, j: (i, 0))],
        out_specs=pl.BlockSpec((tb, PEER_SLOTS), lambda i, j: (i, 0)),
        out_shape=jax.ShapeDtypeStruct((n, PEER_SLOTS), F32),
        compiler_params=_params(("parallel", "arbitrary")),
        name="peer_expert_in",
    )(h1b, ut, a_idx, b_idx)


def _peer_w_kernel(act_ref, g_ref, a_ref, b_ref, w_ref, wt_ref, w3_ref):
    tb = act_ref.shape[0]
    act = act_ref[...]
    gelu = 0.5 * act * (1.0 + lax.erf(act * (2.0 ** -0.5)))
    wt_ref[...] = g_ref[...] * gelu
    sub = lax.broadcasted_iota(jnp.int32, (PEER_N_KEYS, PEER_SLOTS), 0)

    def token(t, _):
        a_row = a_ref[pl.ds(t, 1), :]
        b_row = b_ref[pl.ds(t, 1), :]
        w_row = wt_ref[pl.ds(t, 1), :]
        pt = jnp.where(sub == a_row, w_row, 0.0).astype(BF16)
        qt = jnp.where(sub == b_row, 1.0, 0.0).astype(BF16)
        r0 = pl.multiple_of(t * W_ROW_PITCH, SUBLANES)
        w3_ref[pl.ds(r0, PEER_N_KEYS), :] = _dot_nt(pt, qt)
        return 0

    lax.fori_loop(0, tb, token, 0, unroll=16)
    for i1 in range(PEER_N_KEYS):
        rows = w3_ref[pl.ds(i1, tb, stride=W_ROW_PITCH), :]
        w_ref[:, i1 * PEER_N_KEYS:(i1 + 1) * PEER_N_KEYS] = rows.astype(BF16)


def _peer_w_call(act, gate, a_idx, b_idx, tb):
    n = act.shape[0]
    spec = pl.BlockSpec((tb, PEER_SLOTS), lambda i: (i, 0))
    return pl.pallas_call(
        _peer_w_kernel,
        grid=(n // tb,),
        in_specs=[spec, spec, spec, spec],
        out_specs=pl.BlockSpec((tb, PEER_N_EXPERTS), lambda i: (i, 0)),
        out_shape=jax.ShapeDtypeStruct((n, PEER_N_EXPERTS), BF16),
        scratch_shapes=[pltpu.VMEM((tb, PEER_SLOTS), F32),
                        pltpu.VMEM((tb * W_ROW_PITCH, PEER_N_KEYS), F32)],
        compiler_params=_params(("parallel",)),
        name="peer_expert_weights",
    )(act, gate, a_idx, b_idx)


def _peer_out_kernel(w_ref, v_ref, h1_ref, h1b_ref, p_ref, gw_ref, pw_ref, g2_ref, b2_ref,
                     h2_ref, h2b_ref, acc_ref):
    j = pl.program_id(1)

    @pl.when(j == 0)
    def _():
        acc_ref[...] = jnp.zeros_like(acc_ref)

    acc_ref[...] += _dot(w_ref[...], v_ref[...])

    @pl.when(j == pl.num_programs(1) - 1)
    def _():
        gate = jax.nn.sigmoid(_dot(h1b_ref[...], gw_ref[...]))
        ple = gate * _dot(p_ref[...].astype(BF16), pw_ref[...])
        y = DEEPNORM_ALPHA * h1_ref[...] + acc_ref[...] + ple
        h2 = _layer_norm_rows(y, g2_ref[...], b2_ref[...])
        h2_ref[...] = h2
        h2b_ref[...] = h2.astype(BF16)


def _peer_out_call(w, v, h1, h1b, p, gw, pw, g2, b2, tm, tk):
    n, d = h1.shape
    row = lambda a: a.reshape(1, -1)
    return pl.pallas_call(
        _peer_out_kernel,
        grid=(n // tm, PEER_N_EXPERTS // tk),
        in_specs=[pl.BlockSpec((tm, tk), lambda i, j: (i, j)),
                  pl.BlockSpec((tk, d), lambda i, j: (j, 0)),
                  pl.BlockSpec((tm, d), lambda i, j: (i, 0)),
                  pl.BlockSpec((tm, d), lambda i, j: (i, 0)),
                  pl.BlockSpec((tm, PLE_DIM), lambda i, j: (i, 0)),
                  pl.BlockSpec(gw.shape, lambda i, j: (0, 0)),
                  pl.BlockSpec(pw.shape, lambda i, j: (0, 0)),
                  pl.BlockSpec((1, d), lambda i, j: (0, 0)),
                  pl.BlockSpec((1, d), lambda i, j: (0, 0))],
        out_specs=[pl.BlockSpec((tm, d), lambda i, j: (i, 0)),
                   pl.BlockSpec((tm, d), lambda i, j: (i, 0))],
        out_shape=[jax.ShapeDtypeStruct((n, d), F32), jax.ShapeDtypeStruct((n, d), BF16)],
        scratch_shapes=[pltpu.VMEM((tm, d), F32)],
        compiler_params=_params(("parallel", "arbitrary")),
        name="peer_out_ple_ln2",
    )(w, v, h1, h1b, p, gw, pw, row(g2), row(b2))


def _tiles(n, seq):
    pick = lambda pref, total: pref if total % pref == 0 else total
    return dict(
        rows=pick(512, n),
        tq=pick(512, seq), wide=pick(1024, seq),
        act_tb=pick(1024, n), act_ec=2048,
        w_tb=pick(128, n),
        out_tm=pick(512, n), out_tk=4096,
    )


def kernel(x, p, emb_ln_g, emb_ln_b, w_in, fox_fb, fox_norm_g, hg_lb_logits, hg_norm_g, w_out,
           ln1_g, ln1_b, peer_wq, peer_keys, peer_u, peer_v, ple_gate_w, ple_w, ln2_g, ln2_b):
    batch, seq, d = x.shape
    n = batch * seq
    assert n % SLAB_TOKENS == 0, "token count must be a multiple of the retrieval block"
    tl = _tiles(n, seq)
    h, hb = _ln_call(x.reshape(n, d), emb_ln_g, emb_ln_b, tl["rows"])
    scale = FOX_HEAD_DIM ** -0.5 * LOG2E
    o_ff = 3 * FOX_WIDTH
    o_hg = o_ff + FOX_HEADS
    for i in range(DEPTH):
        wi = w_in[i]
        wqkv = jnp.concatenate([wi[:, :FOX_WIDTH] * scale, wi[:, FOX_WIDTH:o_ff]], axis=1).astype(BF16)
        wff = jnp.pad(wi[:, o_ff:o_hg], ((0, 0), (0, LANES - FOX_HEADS))).astype(BF16)
        whg = wi[:, o_hg:].astype(BF16)
        qkv, ff, hgp = _inproj_call(hb, wqkv, wff, whg, tl["rows"])
        ff_t = ff[:, :FOX_HEADS].reshape(batch, seq, FOX_HEADS).transpose(0, 2, 1).reshape(batch * FOX_HEADS, seq)
        fb_rows = jnp.tile(fox_fb[i], batch).reshape(batch * FOX_HEADS, 1)
        c = _fox_cumsum_call(ff_t, fb_rows).reshape(batch, FOX_HEADS // 2, 2, seq)
        fox_o = _fox_call(qkv.reshape(batch, seq, 3 * FOX_WIDTH), c, batch, seq, tl["tq"], tl["wide"])
        lbl = jnp.concatenate([hg_lb_logits.astype(F32), jnp.full((1, HG_WIDTH), float(i), F32)], axis=0)
        ohg = _hgrn_call(hgp.reshape(batch, seq, 4 * HG_WIDTH), lbl, batch, seq)
        h1, h1b = _mix_call(fox_o.reshape(n, FOX_WIDTH), ohg.reshape(n, HG_WIDTH), hgp, h,
                            fox_norm_g[i], hg_norm_g[i], w_out[i].astype(BF16), ln1_g[i], ln1_b[i], tl["rows"])
        keys = peer_keys[i].reshape(PEER_HEADS * 2, PEER_N_KEYS, PEER_HALF_DIM).astype(BF16)
        a_idx, b_idx, gate = _peerq_call(h1b, peer_wq[i].astype(BF16), keys)
        ut = peer_u[i].astype(BF16).T
        act = _peer_act_call(h1b, ut, a_idx, b_idx, tl["act_tb"], tl["act_ec"])
        w = _peer_w_call(act, gate, a_idx, b_idx, tl["w_tb"])
        h, hb = _peer_out_call(w, peer_v[i].astype(BF16), h1, h1b, p[i].reshape(n, PLE_DIM),
                               ple_gate_w[i].astype(BF16), ple_w[i].astype(BF16), ln2_g[i], ln2_b[i],
                               tl["out_tm"], tl["out_tk"])
    return h.reshape(batch, seq, d)
```

```python
import functools

import jax
import jax.numpy as jnp
from jax import lax
from jax.experimental import pallas as pl
from jax.experimental.pallas import tpu as pltpu

D_MODEL = 1024
DEPTH = 2
PLE_DIM = 256
FOX_HEADS = 8
FOX_HEAD_DIM = 64
FOX_WIDTH = FOX_HEADS * FOX_HEAD_DIM
HG_HEADS = 4
HG_KEY_DIM = 128
HG_VAL_DIM = 128
HG_WIDTH = HG_HEADS * HG_VAL_DIM
PEER_HEADS = 8
PEER_N_KEYS = 128
PEER_N_EXPERTS = PEER_N_KEYS * PEER_N_KEYS
PEER_HALF_DIM = 128
PEER_TOPK = 16
PEER_SLOTS = PEER_HEADS * PEER_TOPK
DEEPNORM_ALPHA = (2 * DEPTH) ** 0.25
LN_EPS = 1e-5
RMS_EPS = 1e-6
MASK_VALUE = -1e30
LOG2E = 1.4426950408889634
LB_FLOOR = 1e-30

LANES = 128
SUBLANES = 8
VMEM_LIMIT_BYTES = 56 * 1024 * 1024

FOX_PAIRS_PER_STEP = 1
HG_CHUNK = 16
HG_CUMSUM_ROWS = 128
W_ROW_PITCH = PEER_N_KEYS + SUBLANES
W_TOKEN_GROUP = 16
W_GROUP_BUFFERS = 1

F32 = jnp.float32
BF16 = jnp.bfloat16


def _params(sem):
    return pltpu.CompilerParams(dimension_semantics=sem, vmem_limit_bytes=VMEM_LIMIT_BYTES)


def _dot(a, b):
    return jnp.dot(a, b, preferred_element_type=F32)


def _dot_nt(a, b):
    return lax.dot_general(a, b, (((1,), (1,)), ((), ())), preferred_element_type=F32)


def _layer_norm_rows(y, g, b):
    mu = jnp.mean(y, axis=-1, keepdims=True)
    d = y - mu
    var = jnp.mean(d * d, axis=-1, keepdims=True)
    return d * lax.rsqrt(var + LN_EPS) * g + b


def _ln_kernel(x_ref, g_ref, b_ref, h_ref, hb_ref):
    h = _layer_norm_rows(x_ref[...], g_ref[...], b_ref[...])
    h_ref[...] = h
    hb_ref[...] = h.astype(BF16)


def _ln_call(x, g, b, tm):
    n, d = x.shape
    return pl.pallas_call(
        _ln_kernel,
        grid=(n // tm,),
        in_specs=[pl.BlockSpec((tm, d), lambda i: (i, 0)),
                  pl.BlockSpec((1, d), lambda i: (0, 0)),
                  pl.BlockSpec((1, d), lambda i: (0, 0))],
        out_specs=[pl.BlockSpec((tm, d), lambda i: (i, 0)),
                   pl.BlockSpec((tm, d), lambda i: (i, 0))],
        out_shape=[jax.ShapeDtypeStruct((n, d), F32), jax.ShapeDtypeStruct((n, d), BF16)],
        compiler_params=_params(("parallel",)),
        name="emb_ln",
    )(x, g.reshape(1, d), b.reshape(1, d))


def _inproj_kernel(h_ref, wqkv_ref, wff_ref, whg_ref, qkv_ref, ff_ref, hg_ref):
    h = h_ref[...]
    qkv_ref[...] = _dot(h, wqkv_ref[...]).astype(BF16)
    ff_ref[...] = _dot(h, wff_ref[...])
    hg_ref[...] = _dot(h, whg_ref[...])


def _inproj_call(hb, wqkv, wff, whg, tm):
    n, d = hb.shape
    return pl.pallas_call(
        _inproj_kernel,
        grid=(n // tm,),
        in_specs=[pl.BlockSpec((tm, d), lambda i: (i, 0)),
                  pl.BlockSpec(wqkv.shape, lambda i: (0, 0)),
                  pl.BlockSpec(wff.shape, lambda i: (0, 0)),
                  pl.BlockSpec(whg.shape, lambda i: (0, 0))],
        out_specs=[pl.BlockSpec((tm, wqkv.shape[1]), lambda i: (i, 0)),
                   pl.BlockSpec((tm, wff.shape[1]), lambda i: (i, 0)),
                   pl.BlockSpec((tm, whg.shape[1]), lambda i: (i, 0))],
        out_shape=[jax.ShapeDtypeStruct((n, wqkv.shape[1]), BF16),
                   jax.ShapeDtypeStruct((n, wff.shape[1]), F32),
                   jax.ShapeDtypeStruct((n, whg.shape[1]), F32)],
        compiler_params=_params(("parallel",)),
        name="in_proj",
    )(hb, wqkv, wff, whg)


def _log_sigmoid(z):
    return jnp.minimum(z, 0.0) - jnp.log(1.0 + jnp.exp(-jnp.abs(z)))


def _fox_cumsum_kernel(ff_ref, fb_ref, c_ref):
    rows, t = ff_ref.shape
    lf = _log_sigmoid(ff_ref[...] + fb_ref[...]) * LOG2E
    r = lax.broadcasted_iota(jnp.int32, (LANES, LANES), 0)
    c = lax.broadcasted_iota(jnp.int32, (LANES, LANES), 1)
    upper = (r <= c).astype(F32)
    carry = jnp.zeros((rows, 1), F32)
    for j in range(t // LANES):
        blk = lf[:, j * LANES:(j + 1) * LANES]
        cs = jnp.dot(blk, upper, preferred_element_type=F32, precision=lax.Precision.HIGHEST)
        cs = cs + carry
        c_ref[:, j * LANES:(j + 1) * LANES] = cs
        carry = cs[:, LANES - 1:LANES]


def _fox_cumsum_call(ff_t, fb_rows):
    rows, t = ff_t.shape
    tr = SUBLANES
    return pl.pallas_call(
        _fox_cumsum_kernel,
        grid=(rows // tr,),
        in_specs=[pl.BlockSpec((tr, t), lambda i: (i, 0)),
                  pl.BlockSpec((tr, 1), lambda i: (i, 0))],
        out_specs=pl.BlockSpec((tr, t), lambda i: (i, 0)),
        out_shape=jax.ShapeDtypeStruct((rows, t), F32),
        compiler_params=_params(("parallel",)),
        name="fox_cumsum",
    )(ff_t, fb_rows)


def _split3(x):
    hi = x.astype(BF16).astype(F32)
    r1 = x - hi
    mid = r1.astype(BF16).astype(F32)
    lo = (r1 - mid).astype(BF16).astype(F32)
    return hi, mid, lo


def _fox_kernel(q_ref, k_ref, v_ref, c_ref, o_ref, kx_ref, vx_ref, ct_ref, *, tq, wide, seq, pairs):
    qi = pl.program_id(2)
    lane = lax.broadcasted_iota(jnp.int32, (1, LANES), 1)
    data = [lane < FOX_HEAD_DIM, lane >= FOX_HEAD_DIM]
    spare = [FOX_HEAD_DIM, 0]
    heads = [(p, h) for p in range(pairs) for h in range(2)]
    slab = lambda p: slice(p * LANES, (p + 1) * LANES)

    def bias_lanes(h, first3, last3):
        out = jnp.zeros_like(first3[0])
        for j in range(3):
            out = jnp.where(lane == spare[h] + j, first3[j], out)
            out = jnp.where(lane == spare[h] + 3 + j, last3[j], out)
        return out

    @pl.when(qi == 0)
    def _():
        for n, (p, h) in enumerate(heads):
            for j in range(seq // LANES):
                rows = slice(j * LANES, (j + 1) * LANES)
                cb = jnp.transpose(jnp.broadcast_to(c_ref[p, h:h + 1, rows], (LANES, LANES)))
                ct_ref[n, rows, :] = cb
                hi, mid, lo = _split3(cb)
                one = jnp.ones_like(hi)
                extra = bias_lanes(h, (-hi, -mid, -lo), (one, one, one))
                kx_ref[n, rows, :] = jnp.where(data[h], k_ref[rows, slab(p)].astype(F32), extra).astype(BF16)
                vx_ref[n, rows, :] = jnp.where(data[h], v_ref[rows, slab(p)].astype(F32), one).astype(BF16)

    q0 = pl.multiple_of(qi * tq, tq)
    qx = []
    for n, (p, h) in enumerate(heads):
        hi, mid, lo = _split3(ct_ref[n, pl.ds(q0, tq), :])
        one = jnp.ones_like(hi)
        q = q_ref[:, slab(p)].astype(F32)
        qx.append(jnp.where(data[h], q, bias_lanes(h, (one, one, one), (hi, mid, lo))).astype(BF16))

    def block(k0, width, masked, carry):
        out = []
        for n in range(len(heads)):
            m_old, acc_old = carry[n]
            s = _dot_nt(qx[n], kx_ref[n, pl.ds(k0, width), :])
            if masked:
                row = lax.broadcasted_iota(jnp.int32, (tq, width), 0) + q0
                col = lax.broadcasted_iota(jnp.int32, (tq, width), 1) + k0
                s = jnp.where(col <= row, s, MASK_VALUE)
            m_new = jnp.maximum(m_old, jnp.max(s, axis=-1, keepdims=True))
            alpha = jnp.exp2(m_old - m_new)
            p = jnp.exp2(s - m_new).astype(BF16)
            acc_new = alpha * acc_old + _dot(p, vx_ref[n, pl.ds(k0, width), :])
            out.append((m_new, acc_new))
        return tuple(out)

    init = tuple((jnp.full((tq, 1), MASK_VALUE, F32), jnp.zeros((tq, LANES), F32)) for _ in heads)
    assert wide in (tq, 2 * tq)
    n_wide = q0 // wide
    carry = lax.fori_loop(
        0, n_wide, lambda j, c: block(pl.multiple_of(j * wide, wide), wide, False, c), init)
    if wide == tq:
        carry = block(q0, tq, True, carry)
    else:
        carry = lax.cond(q0 % wide == 0,
                         lambda c: block(q0, tq, True, c),
                         lambda c: block(pl.multiple_of(n_wide * wide, wide), wide, True, c),
                         carry)
    for p in range(pairs):
        res = []
        for h in range(2):
            acc = carry[2 * p + h][1]
            res.append(acc / pltpu.roll(acc, FOX_HEAD_DIM, axis=1))
        o_ref[:, slab(p)] = jnp.where(data[0], res[0], res[1])


def _fox_call(qkv, c, batch, seq, tq, wide, pairs):
    steps = FOX_HEADS // 2 // pairs
    width = pairs * LANES
    return pl.pallas_call(
        functools.partial(_fox_kernel, tq=tq, wide=wide, seq=seq, pairs=pairs),
        grid=(batch, steps, seq // tq),
        in_specs=[pl.BlockSpec((None, tq, width), lambda b, p, i: (b, i, p)),
                  pl.BlockSpec((None, seq, width), lambda b, p, i: (b, 0, steps + p)),
                  pl.BlockSpec((None, seq, width), lambda b, p, i: (b, 0, 2 * steps + p)),
                  pl.BlockSpec((None, pairs, 2, seq), lambda b, p, i: (b, p, 0, 0))],
        out_specs=pl.BlockSpec((None, tq, width), lambda b, p, i: (b, i, p)),
        out_shape=jax.ShapeDtypeStruct((batch, seq, FOX_WIDTH), F32),
        scratch_shapes=[pltpu.VMEM((2 * pairs, seq, LANES), BF16),
                        pltpu.VMEM((2 * pairs, seq, LANES), BF16),
                        pltpu.VMEM((2 * pairs, seq, LANES), F32)],
        compiler_params=_params(("parallel", "parallel", "arbitrary")),
        name="fox_attention",
    )(qkv, qkv, qkv, c)


def _hgrn_kernel(q_ref, z_ref, v_ref, lbl_ref, o_ref, lf_ref, kk_ref, st_ref, *, seq):
    logits = lbl_ref[...]
    lg = logits[:DEPTH, :]
    mx = jnp.max(lg, axis=0, keepdims=True)
    ex = jnp.exp(lg - mx)
    probs = ex / jnp.sum(ex, axis=0, keepdims=True)
    sel = logits[DEPTH:DEPTH + 1, :]
    lb = jnp.zeros_like(sel)
    run = jnp.zeros_like(sel)
    for i in range(DEPTH):
        run = run + probs[i:i + 1, :]
        lb = jnp.where(sel == float(i), run - probs[0:1, :], lb)
    log_lb = jnp.log(jnp.maximum(lb, LB_FLOOR))
    log_1m = jnp.log(1.0 - lb)

    rr = lax.broadcasted_iota(jnp.int32, (HG_CUMSUM_ROWS, HG_CUMSUM_ROWS), 0)
    cc = lax.broadcasted_iota(jnp.int32, (HG_CUMSUM_ROWS, HG_CUMSUM_ROWS), 1)
    tri = ((rr // HG_CHUNK == cc // HG_CHUNK) & (cc <= rr)).astype(F32).astype(BF16)

    def prep(j, _):
        r0 = pl.multiple_of(j * HG_CUMSUM_ROWS, HG_CUMSUM_ROWS)
        z = z_ref[pl.ds(r0, HG_CUMSUM_ROWS), :]
        a = log_1m + _log_sigmoid(z)
        mm = jnp.maximum(log_lb, a)
        lf = mm + jnp.log(1.0 + jnp.exp(-jnp.abs(log_lb - a)))
        lf_ref[pl.ds(r0, HG_CUMSUM_ROWS), :] = sum(_dot(tri, piece.astype(BF16)) for piece in _split3(lf))
        kk_ref[pl.ds(r0, HG_CUMSUM_ROWS), :] = (1.0 - lb) * jax.nn.sigmoid(-z)
        return 0

    lax.fori_loop(0, seq // HG_CUMSUM_ROWS, prep, 0)
    st_ref[...] = jnp.zeros_like(st_ref)
    row = lax.broadcasted_iota(jnp.int32, (HG_CHUNK, 1), 0)

    def chunk(ci, _):
        r0 = pl.multiple_of(ci * HG_CHUNK, HG_CHUNK)
        b = lf_ref[pl.ds(r0, HG_CHUNK), :]
        q = q_ref[pl.ds(r0, HG_CHUNK), :]
        kk = kk_ref[pl.ds(r0, HG_CHUNK), :]
        v = v_ref[pl.ds(r0, HG_CHUNK), :]
        b_last = b[HG_CHUNK - 1:HG_CHUNK, :]
        qe = (q * jnp.exp(b)).astype(BF16)
        kd = (kk * jnp.exp(b_last - b)).astype(BF16)
        e_last = jnp.exp(b_last)
        outs = []
        for h in range(HG_HEADS):
            sl = slice(h * HG_KEY_DIM, (h + 1) * HG_KEY_DIM)
            st = st_ref[h]
            o = _dot_nt(qe[:, sl], st.astype(BF16))
            bh, qh, kh, vh = b[:, sl], q[:, sl], kk[:, sl], v[:, sl]
            for s in range(HG_CHUNK):
                d = jnp.exp(jnp.where(row >= s, bh - bh[s:s + 1, :], MASK_VALUE))
                a_col = jnp.sum(qh * kh[s:s + 1, :] * d, axis=-1, keepdims=True)
                o = o + a_col * vh[s:s + 1, :]
            upd = lax.dot_general(vh.astype(BF16), kd[:, sl], (((0,), (0,)), ((), ())),
                                  preferred_element_type=F32)
            st_ref[h] = st * e_last[:, sl] + upd
            outs.append(o)
        o_ref[pl.ds(r0, HG_CHUNK), :] = jnp.concatenate(outs, axis=1)
        return 0

    lax.fori_loop(0, seq // HG_CHUNK, chunk, 0)


def _hgrn_call(hgp, lbl, batch, seq):
    return pl.pallas_call(
        functools.partial(_hgrn_kernel, seq=seq),
        grid=(batch,),
        in_specs=[pl.BlockSpec((None, seq, HG_WIDTH), lambda b: (b, 0, 0)),
                  pl.BlockSpec((None, seq, HG_WIDTH), lambda b: (b, 0, 1)),
                  pl.BlockSpec((None, seq, HG_WIDTH), lambda b: (b, 0, 2)),
                  pl.BlockSpec(lbl.shape, lambda b: (0, 0))],
        out_specs=pl.BlockSpec((None, seq, HG_WIDTH), lambda b: (b, 0, 0)),
        out_shape=jax.ShapeDtypeStruct((batch, seq, HG_WIDTH), F32),
        scratch_shapes=[pltpu.VMEM((seq, HG_WIDTH), F32),
                        pltpu.VMEM((seq, HG_WIDTH), F32),
                        pltpu.VMEM((HG_HEADS, HG_VAL_DIM, HG_KEY_DIM), F32)],
        compiler_params=_params(("parallel",)),
        name="hgrn2",
    )(hgp, hgp, hgp, lbl)


def _mix_kernel(fox_ref, ohg_ref, gate_ref, h_ref, gf_ref, gh_ref, wo_ref, g1_ref, b1_ref,
                h1_ref, h1b_ref):
    fox = fox_ref[...]
    fox = fox * lax.rsqrt(jnp.mean(fox * fox, axis=-1, keepdims=True) + RMS_EPS) * gf_ref[...]
    ohg = ohg_ref[...]
    gh = gh_ref[...]
    gate = gate_ref[...]
    parts = [fox.astype(BF16)]
    for h in range(HG_HEADS):
        sl = slice(h * HG_VAL_DIM, (h + 1) * HG_VAL_DIM)
        o = ohg[:, sl]
        o = o * lax.rsqrt(jnp.mean(o * o, axis=-1, keepdims=True) + RMS_EPS) * gh[:, sl]
        g = gate[:, sl]
        parts.append((o * (g * jax.nn.sigmoid(g))).astype(BF16))
    cat = jnp.concatenate(parts, axis=1)
    mix = _dot(cat, wo_ref[...])
    h1 = _layer_norm_rows(DEEPNORM_ALPHA * h_ref[...] + mix, g1_ref[...], b1_ref[...])
    h1_ref[...] = h1
    h1b_ref[...] = h1.astype(BF16)


def _mix_call(fox_o, ohg, hgp, h, gf, gh, wo, g1, b1, tm):
    n, d = h.shape
    row = lambda a: a.reshape(1, -1)
    return pl.pallas_call(
        _mix_kernel,
        grid=(n // tm,),
        in_specs=[pl.BlockSpec((tm, FOX_WIDTH), lambda i: (i, 0)),
                  pl.BlockSpec((tm, HG_WIDTH), lambda i: (i, 0)),
                  pl.BlockSpec((tm, HG_WIDTH), lambda i: (i, 3)),
                  pl.BlockSpec((tm, d), lambda i: (i, 0)),
                  pl.BlockSpec((1, FOX_WIDTH), lambda i: (0, 0)),
                  pl.BlockSpec((1, HG_WIDTH), lambda i: (0, 0)),
                  pl.BlockSpec(wo.shape, lambda i: (0, 0)),
                  pl.BlockSpec((1, d), lambda i: (0, 0)),
                  pl.BlockSpec((1, d), lambda i: (0, 0))],
        out_specs=[pl.BlockSpec((tm, d), lambda i: (i, 0)),
                   pl.BlockSpec((tm, d), lambda i: (i, 0))],
        out_shape=[jax.ShapeDtypeStruct((n, d), F32), jax.ShapeDtypeStruct((n, d), BF16)],
        compiler_params=_params(("parallel",)),
        name="mix_out_ln1",
    )(fox_o, ohg, hgp, h, row(gf), row(gh), wo, row(g1), row(b1))


class _Network:
    def __init__(self):
        self.comparators = []
        self.outputs = []


def _net_merge(net, xs, ys):
    if not xs:
        return list(ys)
    if not ys:
        return list(xs)
    if len(xs) == 1 and len(ys) == 1:
        net.comparators.append((xs[0], ys[0]))
        return [xs[0], ys[0]]
    evens = _net_merge(net, xs[0::2], ys[0::2])
    odds = _net_merge(net, xs[1::2], ys[1::2])
    out = [evens[0]]
    for i, w in enumerate(odds):
        if i + 1 < len(evens):
            net.comparators.append((evens[i + 1], w))
            out += [evens[i + 1], w]
        else:
            out.append(w)
    return out + evens[len(odds) + 1:]


def _net_sort(net, ws):
    if len(ws) <= 1:
        return list(ws)
    mid = len(ws) // 2
    return _net_merge(net, _net_sort(net, ws[:mid]), _net_sort(net, ws[mid:]))


def _top_network(sizes, presorted, keep):
    net = _Network()
    lists, w = [], 0
    for s in sizes:
        ws = list(range(w, w + s))
        w += s
        lists.append(ws if presorted else _net_sort(net, ws))
    while len(lists) > 1:
        lists.sort(key=len)
        lists = [_net_merge(net, lists[0], lists[1])[:keep]] + lists[2:]
    net.outputs = lists[0][:keep]
    needed, kept = set(net.outputs), []
    for a, b in reversed(net.comparators):
        if a in needed or b in needed:
            kept.append((a, b, a in needed, b in needed))
            needed.update((a, b))
    net.comparators = kept[::-1]
    return net


def _run_network(net, wires):
    wires = list(wires)
    for a, b, need_a, need_b in net.comparators:
        (va, ia), (vb, ib) = wires[a], wires[b]
        a_first = (va > vb) | ((va == vb) & (ia < ib))
        if need_a:
            wires[a] = (jnp.where(a_first, va, vb), jnp.where(a_first, ia, ib))
        if need_b:
            wires[b] = (jnp.where(a_first, vb, va), jnp.where(a_first, ib, ia))
    return [wires[w] for w in net.outputs]


def _pair_lists():
    ok = lambda j1, j2: (j1 + 1) * (j2 + 1) <= PEER_TOPK
    lists = []
    for d in range(PEER_TOPK):
        row = [(d, j2) for j2 in range(d, PEER_TOPK) if ok(d, j2)]
        col = [(j1, d) for j1 in range(d + 1, PEER_TOPK) if ok(j1, d)]
        lists += [l for l in (row, col) if l]
    return lists


FLAT_SHIFT = 14
_KEY_NET = _top_network([PEER_TOPK] * (PEER_N_KEYS // PEER_TOPK), presorted=False, keep=PEER_TOPK)
_PAIR_LISTS = _pair_lists()
_PAIR_NET = _top_network([len(l) for l in _PAIR_LISTS], presorted=True, keep=PEER_TOPK)


SLAB_TOKENS = SUBLANES * LANES
SLAB_ROWS = SUBLANES * W_ROW_PITCH


def _peerq_kernel(h_ref, wq_ref, keys_ref, a_ref, b_ref, g_ref,
                  sc_ref, tv_ref, ti_ref, oa_ref, ob_ref, og_ref):
    n_lists = 2 * PEER_HEADS
    q = _dot(h_ref[...], wq_ref[...]).astype(BF16)
    for hp in range(n_lists):
        s = _dot_nt(keys_ref[hp], q[:, hp * PEER_HALF_DIM:(hp + 1) * PEER_HALF_DIM])
        for g in range(SUBLANES):
            sc_ref[pl.ds(hp * SLAB_ROWS + g * W_ROW_PITCH, PEER_N_KEYS), :] = s[:, g * LANES:(g + 1) * LANES]

    def keys_top(hp, _):
        base = hp * SLAB_ROWS
        wires = [(sc_ref[pl.ds(base + n, SUBLANES, stride=W_ROW_PITCH), :],
                  jnp.full((SUBLANES, LANES), n, jnp.int32)) for n in range(PEER_N_KEYS)]
        for j, (v, i) in enumerate(_run_network(_KEY_NET, wires)):
            tv_ref[hp * PEER_TOPK + j] = v
            ti_ref[hp * PEER_TOPK + j] = i
        return 0

    lax.fori_loop(0, n_lists, keys_top, 0)

    def pairs_top(h, _):
        r1 = 2 * h * PEER_TOPK
        r2 = r1 + PEER_TOPK
        s1 = [tv_ref[r1 + j] for j in range(PEER_TOPK)]
        s2 = [tv_ref[r2 + j] for j in range(PEER_TOPK)]
        e1 = [ti_ref[r1 + j] * PEER_N_KEYS for j in range(PEER_TOPK)]
        i2 = [ti_ref[r2 + j] for j in range(PEER_TOPK)]
        wires = [(s1[j1] + s2[j2], e1[j1] + i2[j2] + ((j1 * PEER_TOPK + j2) << FLAT_SHIFT))
                 for lst in _PAIR_LISTS for (j1, j2) in lst]
        top = _run_network(_PAIR_NET, wires)
        es = [jnp.exp(v - top[0][0]) for v, _ in top]
        total = es[0]
        for e in es[1:]:
            total = total + e
        inv = 1.0 / total
        for k, ((_, key), e) in enumerate(zip(top, es)):
            expert = key & ((1 << FLAT_SHIFT) - 1)
            rows = pl.ds(h * PEER_TOPK + k, SUBLANES, stride=W_ROW_PITCH)
            oa_ref[rows, :] = lax.shift_right_logical(expert, 7)
            ob_ref[rows, :] = expert & (PEER_N_KEYS - 1)
            og_ref[rows, :] = e * inv
        return 0

    lax.fori_loop(0, PEER_HEADS, pairs_top, 0)
    for g in range(SUBLANES):
        rows = pl.ds(g * W_ROW_PITCH, PEER_SLOTS)
        toks = slice(g * LANES, (g + 1) * LANES)
        a_ref[toks, :] = jnp.transpose(oa_ref[rows, :])
        b_ref[toks, :] = jnp.transpose(ob_ref[rows, :])
        g_ref[toks, :] = jnp.transpose(og_ref[rows, :])


def _peerq_call(h1b, wq, keys):
    n, d = h1b.shape
    tb = SLAB_TOKENS
    slot = lambda dt: jax.ShapeDtypeStruct((n, PEER_SLOTS), dt)
    return pl.pallas_call(
        _peerq_kernel,
        grid=(n // tb,),
        in_specs=[pl.BlockSpec((tb, d), lambda i: (i, 0)),
                  pl.BlockSpec(wq.shape, lambda i: (0, 0)),
                  pl.BlockSpec(keys.shape, lambda i: (0, 0, 0))],
        out_specs=[pl.BlockSpec((tb, PEER_SLOTS), lambda i: (i, 0))] * 3,
        out_shape=[slot(jnp.int32), slot(jnp.int32), slot(F32)],
        scratch_shapes=[pltpu.VMEM((2 * PEER_HEADS * SLAB_ROWS, LANES), F32),
                        pltpu.VMEM((2 * PEER_HEADS * PEER_TOPK, SUBLANES, LANES), F32),
                        pltpu.VMEM((2 * PEER_HEADS * PEER_TOPK, SUBLANES, LANES), jnp.int32),
                        pltpu.VMEM((SLAB_ROWS, LANES), jnp.int32),
                        pltpu.VMEM((SLAB_ROWS, LANES), jnp.int32),
                        pltpu.VMEM((SLAB_ROWS, LANES), F32)],
        compiler_params=_params(("parallel",)),
        name="peer_retrieve",
    )(h1b, wq, keys)


def _peer_act_kernel(x_ref, ut_ref, a_ref, b_ref, o_ref, *, ec):
    j = pl.program_id(1)

    @pl.when(j == 0)
    def _():
        o_ref[...] = jnp.zeros_like(o_ref)

    act = _dot(x_ref[...], ut_ref[...])
    a_idx = a_ref[...]
    b_idx = b_ref[...]
    acc = o_ref[...]
    groups = ec // PEER_N_KEYS
    for g in range(groups):
        i1 = j * groups + g
        picked = jnp.take_along_axis(act[:, g * PEER_N_KEYS:(g + 1) * PEER_N_KEYS], b_idx, axis=1)
        acc = jnp.where(a_idx == i1, picked, acc)
    o_ref[...] = acc


def _peer_act_call(h1b, ut, a_idx, b_idx, tb, ec):
    n, d = h1b.shape
    return pl.pallas_call(
        functools.partial(_peer_act_kernel, ec=ec),
        grid=(n // tb, PEER_N_EXPERTS // ec),
        in_specs=[pl.BlockSpec((tb, d), lambda i, j: (i, 0)),
                  pl.BlockSpec((d, ec), lambda i, j: (0, j)),
                  pl.BlockSpec((tb, PEER_SLOTS), lambda i, j: (i, 0)),
                  pl.BlockSpec((tb, PEER_SLOTS), lambda i, j: (i, 0))],
        out_specs=pl.BlockSpec((tb, PEER_SLOTS), lambda i, j: (i, 0)),
        out_shape=jax.ShapeDtypeStruct((n, PEER_SLOTS), F32),
        compiler_params=_params(("parallel", "arbitrary")),
        name="peer_expert_in",
    )(h1b, ut, a_idx, b_idx)


def _peer_w_kernel(act_ref, g_ref, a_ref, b_ref, w_ref, wt_ref, *w3_refs):
    tb = act_ref.shape[0]
    act = act_ref[...]
    gelu = 0.5 * act * (1.0 + lax.erf(act * (2.0 ** -0.5)))
    wt_ref[...] = g_ref[...] * gelu
    sub = lax.broadcasted_iota(jnp.int32, (PEER_N_KEYS, PEER_SLOTS), 0)

    for g0 in range(0, tb, W_TOKEN_GROUP):
        w3_ref = w3_refs[(g0 // W_TOKEN_GROUP) % len(w3_refs)]
        for j in range(W_TOKEN_GROUP):
            t = g0 + j
            pt = jnp.where(sub == a_ref[t:t + 1, :], wt_ref[t:t + 1, :], 0.0).astype(BF16)
            qt = jnp.where(sub == b_ref[t:t + 1, :], 1.0, 0.0).astype(BF16)
            w3_ref[j * W_ROW_PITCH:j * W_ROW_PITCH + PEER_N_KEYS, :] = _dot_nt(pt, qt)
        for i1 in range(PEER_N_KEYS):
            rows = w3_ref[pl.ds(i1, W_TOKEN_GROUP, stride=W_ROW_PITCH), :]
            w_ref[g0:g0 + W_TOKEN_GROUP, i1 * PEER_N_KEYS:(i1 + 1) * PEER_N_KEYS] = rows.astype(BF16)


def _peer_w_call(act, gate, a_idx, b_idx, tb):
    n = act.shape[0]
    spec = pl.BlockSpec((tb, PEER_SLOTS), lambda i: (i, 0))
    return pl.pallas_call(
        _peer_w_kernel,
        grid=(n // tb,),
        in_specs=[spec, spec, spec, spec],
        out_specs=pl.BlockSpec((tb, PEER_N_EXPERTS), lambda i: (i, 0)),
        out_shape=jax.ShapeDtypeStruct((n, PEER_N_EXPERTS), BF16),
        scratch_shapes=[pltpu.VMEM((tb, PEER_SLOTS), F32)]
        + [pltpu.VMEM((W_TOKEN_GROUP * W_ROW_PITCH, PEER_N_KEYS), F32)] * W_GROUP_BUFFERS,
        compiler_params=_params(("parallel",)),
        name="peer_expert_weights",
    )(act, gate, a_idx, b_idx)


def _peer_out_kernel(w_ref, v_ref, h1_ref, h1b_ref, p_ref, gw_ref, pw_ref, g2_ref, b2_ref,
                     h2_ref, h2b_ref, acc_ref):
    j = pl.program_id(1)

    @pl.when(j == 0)
    def _():
        acc_ref[...] = jnp.zeros_like(acc_ref)

    acc_ref[...] += _dot(w_ref[...], v_ref[...])

    @pl.when(j == pl.num_programs(1) - 1)
    def _():
        gate = jax.nn.sigmoid(_dot(h1b_ref[...], gw_ref[...]))
        ple = gate * _dot(p_ref[...].astype(BF16), pw_ref[...])
        y = DEEPNORM_ALPHA * h1_ref[...] + acc_ref[...] + ple
        h2 = _layer_norm_rows(y, g2_ref[...], b2_ref[...])
        h2_ref[...] = h2
        h2b_ref[...] = h2.astype(BF16)


def _peer_out_call(w, v, h1, h1b, p, gw, pw, g2, b2, tm, tk):
    n, d = h1.shape
    row = lambda a: a.reshape(1, -1)
    return pl.pallas_call(
        _peer_out_kernel,
        grid=(n // tm, PEER_N_EXPERTS // tk),
        in_specs=[pl.BlockSpec((tm, tk), lambda i, j: (i, j)),
                  pl.BlockSpec((tk, d), lambda i, j: (j, 0)),
                  pl.BlockSpec((tm, d), lambda i, j: (i, 0)),
                  pl.BlockSpec((tm, d), lambda i, j: (i, 0)),
                  pl.BlockSpec((tm, PLE_DIM), lambda i, j: (i, 0)),
                  pl.BlockSpec(gw.shape, lambda i, j: (0, 0)),
                  pl.BlockSpec(pw.shape, lambda i, j: (0, 0)),
                  pl.BlockSpec((1, d), lambda i, j: (0, 0)),
                  pl.BlockSpec((1, d), lambda i, j: (0, 0))],
        out_specs=[pl.BlockSpec((tm, d), lambda i, j: (i, 0)),
                   pl.BlockSpec((tm, d), lambda i, j: (i, 0))],
        out_shape=[jax.ShapeDtypeStruct((n, d), F32), jax.ShapeDtypeStruct((n, d), BF16)],
        scratch_shapes=[pltpu.VMEM((tm, d), F32)],
        compiler_params=_params(("parallel", "arbitrary")),
        name="peer_out_ple_ln2",
    )(w, v, h1, h1b, p, gw, pw, row(g2), row(b2))


def _tiles(n, seq):
    pick = lambda pref, total: pref if total % pref == 0 else total
    return dict(
        rows=pick(512, n),
        tq=pick(512, seq), wide=pick(1024, seq),
        act_tb=pick(1024, n), act_ec=2048,
        w_tb=pick(128, n),
        out_tm=pick(512, n), out_tk=4096,
    )


def kernel(x, p, emb_ln_g, emb_ln_b, w_in, fox_fb, fox_norm_g, hg_lb_logits, hg_norm_g, w_out,
           ln1_g, ln1_b, peer_wq, peer_keys, peer_u, peer_v, ple_gate_w, ple_w, ln2_g, ln2_b):
    batch, seq, d = x.shape
    n = batch * seq
    assert n % SLAB_TOKENS == 0, "token count must be a multiple of the retrieval block"
    tl = _tiles(n, seq)
    h, hb = _ln_call(x.reshape(n, d), emb_ln_g, emb_ln_b, tl["rows"])
    scale = FOX_HEAD_DIM ** -0.5 * LOG2E
    o_ff = 3 * FOX_WIDTH
    o_hg = o_ff + FOX_HEADS
    for i in range(DEPTH):
        wi = w_in[i]
        wqkv = jnp.concatenate([wi[:, :FOX_WIDTH] * scale, wi[:, FOX_WIDTH:o_ff]], axis=1).astype(BF16)
        wff = jnp.pad(wi[:, o_ff:o_hg], ((0, 0), (0, LANES - FOX_HEADS))).astype(BF16)
        whg = wi[:, o_hg:].astype(BF16)
        qkv, ff, hgp = _inproj_call(hb, wqkv, wff, whg, tl["rows"])
        ff_t = ff[:, :FOX_HEADS].reshape(batch, seq, FOX_HEADS).transpose(0, 2, 1).reshape(batch * FOX_HEADS, seq)
        fb_rows = jnp.tile(fox_fb[i], batch).reshape(batch * FOX_HEADS, 1)
        c = _fox_cumsum_call(ff_t, fb_rows).reshape(batch, FOX_HEADS // 2, 2, seq)
        fox_o = _fox_call(qkv.reshape(batch, seq, 3 * FOX_WIDTH), c, batch, seq, tl["tq"], tl["wide"], FOX_PAIRS_PER_STEP)
        lbl = jnp.concatenate([hg_lb_logits.astype(F32), jnp.full((1, HG_WIDTH), float(i), F32)], axis=0)
        ohg = _hgrn_call(hgp.reshape(batch, seq, 4 * HG_WIDTH), lbl, batch, seq)
        h1, h1b = _mix_call(fox_o.reshape(n, FOX_WIDTH), ohg.reshape(n, HG_WIDTH), hgp, h,
                            fox_norm_g[i], hg_norm_g[i], w_out[i].astype(BF16), ln1_g[i], ln1_b[i], tl["rows"])
        keys = peer_keys[i].reshape(PEER_HEADS * 2, PEER_N_KEYS, PEER_HALF_DIM).astype(BF16)
        a_idx, b_idx, gate = _peerq_call(h1b, peer_wq[i].astype(BF16), keys)
        ut = peer_u[i].astype(BF16).T
        act = _peer_act_call(h1b, ut, a_idx, b_idx, tl["act_tb"], tl["act_ec"])
        w = _peer_w_call(act, gate, a_idx, b_idx, tl["w_tb"])
        h, hb = _peer_out_call(w, peer_v[i].astype(BF16), h1, h1b, p[i].reshape(n, PLE_DIM),
                               ple_gate_w[i].astype(BF16), ple_w[i].astype(BF16), ln2_g[i], ln2_b[i],
                               tl["out_tm"], tl["out_tk"])
    return h.reshape(batch, seq, d)
```

```python
import functools

import jax
import jax.numpy as jnp
from jax import lax
from jax.experimental import pallas as pl
from jax.experimental.pallas import tpu as pltpu

D_MODEL = 1024
DEPTH = 2
PLE_DIM = 256
FOX_HEADS = 8
FOX_HEAD_DIM = 64
FOX_WIDTH = FOX_HEADS * FOX_HEAD_DIM
HG_HEADS = 4
HG_KEY_DIM = 128
HG_VAL_DIM = 128
HG_WIDTH = HG_HEADS * HG_VAL_DIM
PEER_HEADS = 8
PEER_N_KEYS = 128
PEER_N_EXPERTS = PEER_N_KEYS * PEER_N_KEYS
PEER_HALF_DIM = 128
PEER_TOPK = 16
PEER_SLOTS = PEER_HEADS * PEER_TOPK
DEEPNORM_ALPHA = (2 * DEPTH) ** 0.25
LN_EPS = 1e-5
RMS_EPS = 1e-6
MASK_VALUE = -1e30
LOG2E = 1.4426950408889634
LB_FLOOR = 1e-30

LANES = 128
SUBLANES = 8
VMEM_LIMIT_BYTES = 56 * 1024 * 1024

FOX_PAIRS_PER_STEP = 1
HG_CHUNK = 16
HG_CUMSUM_ROWS = 128
W_ROW_PITCH = PEER_N_KEYS + SUBLANES
W_TOKEN_GROUP = 16
W_GROUP_BUFFERS = 1

F32 = jnp.float32
BF16 = jnp.bfloat16


def _params(sem):
    return pltpu.CompilerParams(dimension_semantics=sem, vmem_limit_bytes=VMEM_LIMIT_BYTES)


def _dot(a, b):
    return jnp.dot(a, b, preferred_element_type=F32)


def _dot_nt(a, b):
    return lax.dot_general(a, b, (((1,), (1,)), ((), ())), preferred_element_type=F32)


def _layer_norm_rows(y, g, b):
    mu = jnp.mean(y, axis=-1, keepdims=True)
    d = y - mu
    var = jnp.mean(d * d, axis=-1, keepdims=True)
    return d * lax.rsqrt(var + LN_EPS) * g + b


def _ln_kernel(x_ref, g_ref, b_ref, h_ref, hb_ref):
    h = _layer_norm_rows(x_ref[...], g_ref[...], b_ref[...])
    h_ref[...] = h
    hb_ref[...] = h.astype(BF16)


def _ln_call(x, g, b, tm):
    n, d = x.shape
    return pl.pallas_call(
        _ln_kernel,
        grid=(n // tm,),
        in_specs=[pl.BlockSpec((tm, d), lambda i: (i, 0)),
                  pl.BlockSpec((1, d), lambda i: (0, 0)),
                  pl.BlockSpec((1, d), lambda i: (0, 0))],
        out_specs=[pl.BlockSpec((tm, d), lambda i: (i, 0)),
                   pl.BlockSpec((tm, d), lambda i: (i, 0))],
        out_shape=[jax.ShapeDtypeStruct((n, d), F32), jax.ShapeDtypeStruct((n, d), BF16)],
        compiler_params=_params(("parallel",)),
        name="emb_ln",
    )(x, g.reshape(1, d), b.reshape(1, d))


def _inproj_kernel(h_ref, wqkv_ref, wff_ref, whg_ref, qkv_ref, ff_ref, hg_ref):
    h = h_ref[...]
    qkv_ref[...] = _dot(h, wqkv_ref[...]).astype(BF16)
    ff_ref[...] = _dot(h, wff_ref[...])
    hg_ref[...] = _dot(h, whg_ref[...])


def _inproj_call(hb, wqkv, wff, whg, tm):
    n, d = hb.shape
    return pl.pallas_call(
        _inproj_kernel,
        grid=(n // tm,),
        in_specs=[pl.BlockSpec((tm, d), lambda i: (i, 0)),
                  pl.BlockSpec(wqkv.shape, lambda i: (0, 0)),
                  pl.BlockSpec(wff.shape, lambda i: (0, 0)),
                  pl.BlockSpec(whg.shape, lambda i: (0, 0))],
        out_specs=[pl.BlockSpec((tm, wqkv.shape[1]), lambda i: (i, 0)),
                   pl.BlockSpec((tm, wff.shape[1]), lambda i: (i, 0)),
                   pl.BlockSpec((tm, whg.shape[1]), lambda i: (i, 0))],
        out_shape=[jax.ShapeDtypeStruct((n, wqkv.shape[1]), BF16),
                   jax.ShapeDtypeStruct((n, wff.shape[1]), F32),
                   jax.ShapeDtypeStruct((n, whg.shape[1]), F32)],
        compiler_params=_params(("parallel",)),
        name="in_proj",
    )(hb, wqkv, wff, whg)


def _log_sigmoid(z):
    return jnp.minimum(z, 0.0) - jnp.log(1.0 + jnp.exp(-jnp.abs(z)))


def _fox_cumsum_kernel(ff_ref, fb_ref, c_ref):
    rows, t = ff_ref.shape
    lf = _log_sigmoid(ff_ref[...] + fb_ref[...]) * LOG2E
    r = lax.broadcasted_iota(jnp.int32, (LANES, LANES), 0)
    c = lax.broadcasted_iota(jnp.int32, (LANES, LANES), 1)
    upper = (r <= c).astype(F32)
    carry = jnp.zeros((rows, 1), F32)
    for j in range(t // LANES):
        blk = lf[:, j * LANES:(j + 1) * LANES]
        cs = jnp.dot(blk, upper, preferred_element_type=F32, precision=lax.Precision.HIGHEST)
        cs = cs + carry
        c_ref[:, j * LANES:(j + 1) * LANES] = cs
        carry = cs[:, LANES - 1:LANES]


def _fox_cumsum_call(ff_t, fb_rows):
    rows, t = ff_t.shape
    tr = SUBLANES
    return pl.pallas_call(
        _fox_cumsum_kernel,
        grid=(rows // tr,),
        in_specs=[pl.BlockSpec((tr, t), lambda i: (i, 0)),
                  pl.BlockSpec((tr, 1), lambda i: (i, 0))],
        out_specs=pl.BlockSpec((tr, t), lambda i: (i, 0)),
        out_shape=jax.ShapeDtypeStruct((rows, t), F32),
        compiler_params=_params(("parallel",)),
        name="fox_cumsum",
    )(ff_t, fb_rows)


def _split3(x):
    hi = x.astype(BF16).astype(F32)
    r1 = x - hi
    mid = r1.astype(BF16).astype(F32)
    lo = (r1 - mid).astype(BF16).astype(F32)
    return hi, mid, lo


def _fox_kernel(q_ref, k_ref, v_ref, c_ref, o_ref, kx_ref, vx_ref, ct_ref, *, tq, wide, seq, pairs):
    qi = pl.program_id(2)
    lane = lax.broadcasted_iota(jnp.int32, (1, LANES), 1)
    data = [lane < FOX_HEAD_DIM, lane >= FOX_HEAD_DIM]
    spare = [FOX_HEAD_DIM, 0]
    heads = [(p, h) for p in range(pairs) for h in range(2)]
    slab = lambda p: slice(p * LANES, (p + 1) * LANES)

    def bias_lanes(h, first3, last3):
        out = jnp.zeros_like(first3[0])
        for j in range(3):
            out = jnp.where(lane == spare[h] + j, first3[j], out)
            out = jnp.where(lane == spare[h] + 3 + j, last3[j], out)
        return out

    @pl.when(qi == 0)
    def _():
        for n, (p, h) in enumerate(heads):
            for j in range(seq // LANES):
                rows = slice(j * LANES, (j + 1) * LANES)
                cb = jnp.transpose(jnp.broadcast_to(c_ref[p, h:h + 1, rows], (LANES, LANES)))
                ct_ref[n, rows, :] = cb
                hi, mid, lo = _split3(cb)
                one = jnp.ones_like(hi)
                extra = bias_lanes(h, (-hi, -mid, -lo), (one, one, one))
                kx_ref[n, rows, :] = jnp.where(data[h], k_ref[rows, slab(p)].astype(F32), extra).astype(BF16)
                vx_ref[n, rows, :] = jnp.where(data[h], v_ref[rows, slab(p)].astype(F32), one).astype(BF16)

    q0 = pl.multiple_of(qi * tq, tq)
    qx = []
    for n, (p, h) in enumerate(heads):
        hi, mid, lo = _split3(ct_ref[n, pl.ds(q0, tq), :])
        one = jnp.ones_like(hi)
        q = q_ref[:, slab(p)].astype(F32)
        qx.append(jnp.where(data[h], q, bias_lanes(h, (one, one, one), (hi, mid, lo))).astype(BF16))

    def block(k0, width, masked, carry):
        out = []
        for n in range(len(heads)):
            m_old, acc_old = carry[n]
            s = _dot_nt(qx[n], kx_ref[n, pl.ds(k0, width), :])
            if masked:
                row = lax.broadcasted_iota(jnp.int32, (tq, width), 0) + q0
                col = lax.broadcasted_iota(jnp.int32, (tq, width), 1) + k0
                s = jnp.where(col <= row, s, MASK_VALUE)
            m_new = jnp.maximum(m_old, jnp.max(s, axis=-1, keepdims=True))
            alpha = jnp.exp2(m_old - m_new)
            p = jnp.exp2(s - m_new).astype(BF16)
            acc_new = alpha * acc_old + _dot(p, vx_ref[n, pl.ds(k0, width), :])
            out.append((m_new, acc_new))
        return tuple(out)

    init = tuple((jnp.full((tq, 1), MASK_VALUE, F32), jnp.zeros((tq, LANES), F32)) for _ in heads)
    assert wide in (tq, 2 * tq)
    n_wide = q0 // wide
    carry = lax.fori_loop(
        0, n_wide, lambda j, c: block(pl.multiple_of(j * wide, wide), wide, False, c), init)
    if wide == tq:
        carry = block(q0, tq, True, carry)
    else:
        carry = lax.cond(q0 % wide == 0,
                         lambda c: block(q0, tq, True, c),
                         lambda c: block(pl.multiple_of(n_wide * wide, wide), wide, True, c),
                         carry)
    for p in range(pairs):
        res = []
        for h in range(2):
            acc = carry[2 * p + h][1]
            res.append(acc / pltpu.roll(acc, FOX_HEAD_DIM, axis=1))
        o_ref[:, slab(p)] = jnp.where(data[0], res[0], res[1])


def _fox_call(qkv, c, batch, seq, tq, wide, pairs):
    steps = FOX_HEADS // 2 // pairs
    width = pairs * LANES
    return pl.pallas_call(
        functools.partial(_fox_kernel, tq=tq, wide=wide, seq=seq, pairs=pairs),
        grid=(batch, steps, seq // tq),
        in_specs=[pl.BlockSpec((None, tq, width), lambda b, p, i: (b, i, p)),
                  pl.BlockSpec((None, seq, width), lambda b, p, i: (b, 0, steps + p)),
                  pl.BlockSpec((None, seq, width), lambda b, p, i: (b, 0, 2 * steps + p)),
                  pl.BlockSpec((None, pairs, 2, seq), lambda b, p, i: (b, p, 0, 0))],
        out_specs=pl.BlockSpec((None, tq, width), lambda b, p, i: (b, i, p)),
        out_shape=jax.ShapeDtypeStruct((batch, seq, FOX_WIDTH), F32),
        scratch_shapes=[pltpu.VMEM((2 * pairs, seq, LANES), BF16),
                        pltpu.VMEM((2 * pairs, seq, LANES), BF16),
                        pltpu.VMEM((2 * pairs, seq, LANES), F32)],
        compiler_params=_params(("parallel", "parallel", "arbitrary")),
        name="fox_attention",
    )(qkv, qkv, qkv, c)


def _hgrn_kernel(q_ref, z_ref, v_ref, lbl_ref, o_ref, lf_ref, kk_ref, st_ref, *, seq):
    logits = lbl_ref[...]
    lg = logits[:DEPTH, :]
    mx = jnp.max(lg, axis=0, keepdims=True)
    ex = jnp.exp(lg - mx)
    probs = ex / jnp.sum(ex, axis=0, keepdims=True)
    sel = logits[DEPTH:DEPTH + 1, :]
    lb = jnp.zeros_like(sel)
    run = jnp.zeros_like(sel)
    for i in range(DEPTH):
        run = run + probs[i:i + 1, :]
        lb = jnp.where(sel == float(i), run - probs[0:1, :], lb)
    log_lb = jnp.log(jnp.maximum(lb, LB_FLOOR))
    log_1m = jnp.log(1.0 - lb)

    rr = lax.broadcasted_iota(jnp.int32, (HG_CUMSUM_ROWS, HG_CUMSUM_ROWS), 0)
    cc = lax.broadcasted_iota(jnp.int32, (HG_CUMSUM_ROWS, HG_CUMSUM_ROWS), 1)
    tri = ((rr // HG_CHUNK == cc // HG_CHUNK) & (cc <= rr)).astype(F32).astype(BF16)

    def prep(j, _):
        r0 = pl.multiple_of(j * HG_CUMSUM_ROWS, HG_CUMSUM_ROWS)
        z = z_ref[pl.ds(r0, HG_CUMSUM_ROWS), :]
        a = log_1m + _log_sigmoid(z)
        mm = jnp.maximum(log_lb, a)
        lf = mm + jnp.log(1.0 + jnp.exp(-jnp.abs(log_lb - a)))
        lf_ref[pl.ds(r0, HG_CUMSUM_ROWS), :] = sum(_dot(tri, piece.astype(BF16)) for piece in _split3(lf))
        kk_ref[pl.ds(r0, HG_CUMSUM_ROWS), :] = (1.0 - lb) * jax.nn.sigmoid(-z)
        return 0

    lax.fori_loop(0, seq // HG_CUMSUM_ROWS, prep, 0, unroll=2)
    st_ref[...] = jnp.zeros_like(st_ref)
    row8 = lax.broadcasted_iota(jnp.int32, (SUBLANES, 1), 0)

    def chunk(ci, _):
        r0 = pl.multiple_of(ci * HG_CHUNK, HG_CHUNK)
        b = lf_ref[pl.ds(r0, HG_CHUNK), :]
        q = q_ref[pl.ds(r0, HG_CHUNK), :]
        kk = kk_ref[pl.ds(r0, HG_CHUNK), :]
        v = v_ref[pl.ds(r0, HG_CHUNK), :]
        b_last = b[HG_CHUNK - 1:HG_CHUNK, :]
        qe = (q * jnp.exp(b)).astype(BF16)
        kd = (kk * jnp.exp(b_last - b)).astype(BF16)
        e_last = jnp.exp(b_last)
        outs = []
        for h in range(HG_HEADS):
            sl = slice(h * HG_KEY_DIM, (h + 1) * HG_KEY_DIM)
            st = st_ref[h]
            o = _dot_nt(qe[:, sl], st.astype(BF16))
            bh, qh, kh, vh = b[:, sl], q[:, sl], kk[:, sl], v[:, sl]
            halves = []
            for half in range(HG_CHUNK // SUBLANES):
                rows = slice(half * SUBLANES, (half + 1) * SUBLANES)
                bt, qt, acc = bh[rows], qh[rows], o[rows]
                for s in range((half + 1) * SUBLANES):
                    diff = bt - bh[s:s + 1, :]
                    first_live = s - half * SUBLANES
                    if first_live > 0:
                        diff = jnp.where(row8 >= first_live, diff, MASK_VALUE)
                    a_col = jnp.sum(qt * kh[s:s + 1, :] * jnp.exp(diff), axis=-1, keepdims=True)
                    acc = acc + a_col * vh[s:s + 1, :]
                halves.append(acc)
            o = jnp.concatenate(halves, axis=0)
            upd = lax.dot_general(vh.astype(BF16), kd[:, sl], (((0,), (0,)), ((), ())),
                                  preferred_element_type=F32)
            st_ref[h] = st * e_last[:, sl] + upd
            outs.append(o)
        o_ref[pl.ds(r0, HG_CHUNK), :] = jnp.concatenate(outs, axis=1)
        return 0

    lax.fori_loop(0, seq // HG_CHUNK, chunk, 0, unroll=4)


def _hgrn_call(hgp, lbl, batch, seq):
    return pl.pallas_call(
        functools.partial(_hgrn_kernel, seq=seq),
        grid=(batch,),
        in_specs=[pl.BlockSpec((None, seq, HG_WIDTH), lambda b: (b, 0, 0)),
                  pl.BlockSpec((None, seq, HG_WIDTH), lambda b: (b, 0, 1)),
                  pl.BlockSpec((None, seq, HG_WIDTH), lambda b: (b, 0, 2)),
                  pl.BlockSpec(lbl.shape, lambda b: (0, 0))],
        out_specs=pl.BlockSpec((None, seq, HG_WIDTH), lambda b: (b, 0, 0)),
        out_shape=jax.ShapeDtypeStruct((batch, seq, HG_WIDTH), F32),
        scratch_shapes=[pltpu.VMEM((seq, HG_WIDTH), F32),
                        pltpu.VMEM((seq, HG_WIDTH), F32),
                        pltpu.VMEM((HG_HEADS, HG_VAL_DIM, HG_KEY_DIM), F32)],
        compiler_params=_params(("parallel",)),
        name="hgrn2",
    )(hgp, hgp, hgp, lbl)


def _mix_kernel(fox_ref, ohg_ref, gate_ref, h_ref, gf_ref, gh_ref, wo_ref, g1_ref, b1_ref,
                h1_ref, h1b_ref):
    fox = fox_ref[...]
    fox = fox * lax.rsqrt(jnp.mean(fox * fox, axis=-1, keepdims=True) + RMS_EPS) * gf_ref[...]
    ohg = ohg_ref[...]
    gh = gh_ref[...]
    gate = gate_ref[...]
    parts = [fox.astype(BF16)]
    for h in range(HG_HEADS):
        sl = slice(h * HG_VAL_DIM, (h + 1) * HG_VAL_DIM)
        o = ohg[:, sl]
        o = o * lax.rsqrt(jnp.mean(o * o, axis=-1, keepdims=True) + RMS_EPS) * gh[:, sl]
        g = gate[:, sl]
        parts.append((o * (g * jax.nn.sigmoid(g))).astype(BF16))
    cat = jnp.concatenate(parts, axis=1)
    mix = _dot(cat, wo_ref[...])
    h1 = _layer_norm_rows(DEEPNORM_ALPHA * h_ref[...] + mix, g1_ref[...], b1_ref[...])
    h1_ref[...] = h1
    h1b_ref[...] = h1.astype(BF16)


def _mix_call(fox_o, ohg, hgp, h, gf, gh, wo, g1, b1, tm):
    n, d = h.shape
    row = lambda a: a.reshape(1, -1)
    return pl.pallas_call(
        _mix_kernel,
        grid=(n // tm,),
        in_specs=[pl.BlockSpec((tm, FOX_WIDTH), lambda i: (i, 0)),
                  pl.BlockSpec((tm, HG_WIDTH), lambda i: (i, 0)),
                  pl.BlockSpec((tm, HG_WIDTH), lambda i: (i, 3)),
                  pl.BlockSpec((tm, d), lambda i: (i, 0)),
                  pl.BlockSpec((1, FOX_WIDTH), lambda i: (0, 0)),
                  pl.BlockSpec((1, HG_WIDTH), lambda i: (0, 0)),
                  pl.BlockSpec(wo.shape, lambda i: (0, 0)),
                  pl.BlockSpec((1, d), lambda i: (0, 0)),
                  pl.BlockSpec((1, d), lambda i: (0, 0))],
        out_specs=[pl.BlockSpec((tm, d), lambda i: (i, 0)),
                   pl.BlockSpec((tm, d), lambda i: (i, 0))],
        out_shape=[jax.ShapeDtypeStruct((n, d), F32), jax.ShapeDtypeStruct((n, d), BF16)],
        compiler_params=_params(("parallel",)),
        name="mix_out_ln1",
    )(fox_o, ohg, hgp, h, row(gf), row(gh), wo, row(g1), row(b1))


class _Network:
    def __init__(self):
        self.comparators = []
        self.outputs = []


def _net_merge(net, xs, ys):
    if not xs:
        return list(ys)
    if not ys:
        return list(xs)
    if len(xs) == 1 and len(ys) == 1:
        net.comparators.append((xs[0], ys[0]))
        return [xs[0], ys[0]]
    evens = _net_merge(net, xs[0::2], ys[0::2])
    odds = _net_merge(net, xs[1::2], ys[1::2])
    out = [evens[0]]
    for i, w in enumerate(odds):
        if i + 1 < len(evens):
            net.comparators.append((evens[i + 1], w))
            out += [evens[i + 1], w]
        else:
            out.append(w)
    return out + evens[len(odds) + 1:]


def _net_sort(net, ws):
    if len(ws) <= 1:
        return list(ws)
    mid = len(ws) // 2
    return _net_merge(net, _net_sort(net, ws[:mid]), _net_sort(net, ws[mid:]))


def _top_network(sizes, presorted, keep):
    net = _Network()
    lists, w = [], 0
    for s in sizes:
        ws = list(range(w, w + s))
        w += s
        lists.append(ws if presorted else _net_sort(net, ws))
    while len(lists) > 1:
        lists.sort(key=len)
        lists = [_net_merge(net, lists[0], lists[1])[:keep]] + lists[2:]
    net.outputs = lists[0][:keep]
    needed, kept = set(net.outputs), []
    for a, b in reversed(net.comparators):
        if a in needed or b in needed:
            kept.append((a, b, a in needed, b in needed))
            needed.update((a, b))
    net.comparators = kept[::-1]
    return net


def _run_network(net, wires):
    wires = list(wires)
    for a, b, need_a, need_b in net.comparators:
        (va, ia), (vb, ib) = wires[a], wires[b]
        a_first = (va > vb) | ((va == vb) & (ia < ib))
        if need_a:
            wires[a] = (jnp.where(a_first, va, vb), jnp.where(a_first, ia, ib))
        if need_b:
            wires[b] = (jnp.where(a_first, vb, va), jnp.where(a_first, ib, ia))
    return [wires[w] for w in net.outputs]


def _pair_lists():
    ok = lambda j1, j2: (j1 + 1) * (j2 + 1) <= PEER_TOPK
    lists = []
    for d in range(PEER_TOPK):
        row = [(d, j2) for j2 in range(d, PEER_TOPK) if ok(d, j2)]
        col = [(j1, d) for j1 in range(d + 1, PEER_TOPK) if ok(j1, d)]
        lists += [l for l in (row, col) if l]
    return lists


FLAT_SHIFT = 14
_KEY_NET = _top_network([PEER_TOPK] * (PEER_N_KEYS // PEER_TOPK), presorted=False, keep=PEER_TOPK)
_PAIR_LISTS = _pair_lists()
_PAIR_NET = _top_network([len(l) for l in _PAIR_LISTS], presorted=True, keep=PEER_TOPK)


SLAB_TOKENS = SUBLANES * LANES
SLAB_ROWS = SUBLANES * W_ROW_PITCH


def _peerq_kernel(h_ref, wq_ref, keys_ref, a_ref, b_ref, g_ref,
                  sc_ref, tv_ref, ti_ref, oa_ref, ob_ref, og_ref):
    n_lists = 2 * PEER_HEADS
    q = _dot(h_ref[...], wq_ref[...]).astype(BF16)
    for hp in range(n_lists):
        s = _dot_nt(keys_ref[hp], q[:, hp * PEER_HALF_DIM:(hp + 1) * PEER_HALF_DIM])
        for g in range(SUBLANES):
            sc_ref[pl.ds(hp * SLAB_ROWS + g * W_ROW_PITCH, PEER_N_KEYS), :] = s[:, g * LANES:(g + 1) * LANES]

    def keys_top(hp, _):
        base = hp * SLAB_ROWS
        wires = [(sc_ref[pl.ds(base + n, SUBLANES, stride=W_ROW_PITCH), :],
                  jnp.full((SUBLANES, LANES), n, jnp.int32)) for n in range(PEER_N_KEYS)]
        for j, (v, i) in enumerate(_run_network(_KEY_NET, wires)):
            tv_ref[hp * PEER_TOPK + j] = v
            ti_ref[hp * PEER_TOPK + j] = i
        return 0

    lax.fori_loop(0, n_lists, keys_top, 0)

    def pairs_top(h, _):
        r1 = 2 * h * PEER_TOPK
        r2 = r1 + PEER_TOPK
        s1 = [tv_ref[r1 + j] for j in range(PEER_TOPK)]
        s2 = [tv_ref[r2 + j] for j in range(PEER_TOPK)]
        e1 = [ti_ref[r1 + j] * PEER_N_KEYS for j in range(PEER_TOPK)]
        i2 = [ti_ref[r2 + j] for j in range(PEER_TOPK)]
        wires = [(s1[j1] + s2[j2], e1[j1] + i2[j2] + ((j1 * PEER_TOPK + j2) << FLAT_SHIFT))
                 for lst in _PAIR_LISTS for (j1, j2) in lst]
        top = _run_network(_PAIR_NET, wires)
        es = [jnp.exp(v - top[0][0]) for v, _ in top]
        total = es[0]
        for e in es[1:]:
            total = total + e
        inv = 1.0 / total
        for k, ((_, key), e) in enumerate(zip(top, es)):
            expert = key & ((1 << FLAT_SHIFT) - 1)
            rows = pl.ds(h * PEER_TOPK + k, SUBLANES, stride=W_ROW_PITCH)
            oa_ref[rows, :] = lax.shift_right_logical(expert, 7)
            ob_ref[rows, :] = expert & (PEER_N_KEYS - 1)
            og_ref[rows, :] = e * inv
        return 0

    lax.fori_loop(0, PEER_HEADS, pairs_top, 0)
    for g in range(SUBLANES):
        rows = pl.ds(g * W_ROW_PITCH, PEER_SLOTS)
        toks = slice(g * LANES, (g + 1) * LANES)
        a_ref[toks, :] = jnp.transpose(oa_ref[rows, :])
        b_ref[toks, :] = jnp.transpose(ob_ref[rows, :])
        g_ref[toks, :] = jnp.transpose(og_ref[rows, :])


def _peerq_call(h1b, wq, keys):
    n, d = h1b.shape
    tb = SLAB_TOKENS
    slot = lambda dt: jax.ShapeDtypeStruct((n, PEER_SLOTS), dt)
    return pl.pallas_call(
        _peerq_kernel,
        grid=(n // tb,),
        in_specs=[pl.BlockSpec((tb, d), lambda i: (i, 0)),
                  pl.BlockSpec(wq.shape, lambda i: (0, 0)),
                  pl.BlockSpec(keys.shape, lambda i: (0, 0, 0))],
        out_specs=[pl.BlockSpec((tb, PEER_SLOTS), lambda i: (i, 0))] * 3,
        out_shape=[slot(jnp.int32), slot(jnp.int32), slot(F32)],
        scratch_shapes=[pltpu.VMEM((2 * PEER_HEADS * SLAB_ROWS, LANES), F32),
                        pltpu.VMEM((2 * PEER_HEADS * PEER_TOPK, SUBLANES, LANES), F32),
                        pltpu.VMEM((2 * PEER_HEADS * PEER_TOPK, SUBLANES, LANES), jnp.int32),
                        pltpu.VMEM((SLAB_ROWS, LANES), jnp.int32),
                        pltpu.VMEM((SLAB_ROWS, LANES), jnp.int32),
                        pltpu.VMEM((SLAB_ROWS, LANES), F32)],
        compiler_params=_params(("parallel",)),
        name="peer_retrieve",
    )(h1b, wq, keys)


def _peer_act_kernel(x_ref, ut_ref, a_ref, b_ref, o_ref, *, ec):
    j = pl.program_id(1)

    @pl.when(j == 0)
    def _():
        o_ref[...] = jnp.zeros_like(o_ref)

    act = _dot(x_ref[...], ut_ref[...])
    a_idx = a_ref[...]
    b_idx = b_ref[...]
    acc = o_ref[...]
    groups = ec // PEER_N_KEYS
    for g in range(groups):
        i1 = j * groups + g
        picked = jnp.take_along_axis(act[:, g * PEER_N_KEYS:(g + 1) * PEER_N_KEYS], b_idx, axis=1)
        acc = jnp.where(a_idx == i1, picked, acc)
    o_ref[...] = acc


def _peer_act_call(h1b, ut, a_idx, b_idx, tb, ec):
    n, d = h1b.shape
    return pl.pallas_call(
        functools.partial(_peer_act_kernel, ec=ec),
        grid=(n // tb, PEER_N_EXPERTS // ec),
        in_specs=[pl.BlockSpec((tb, d), lambda i, j: (i, 0)),
                  pl.BlockSpec((d, ec), lambda i, j: (0, j)),
                  pl.BlockSpec((tb, PEER_SLOTS), lambda i, j: (i, 0)),
                  pl.BlockSpec((tb, PEER_SLOTS), lambda i, j: (i, 0))],
        out_specs=pl.BlockSpec((tb, PEER_SLOTS), lambda i, j: (i, 0)),
        out_shape=jax.ShapeDtypeStruct((n, PEER_SLOTS), F32),
        compiler_params=_params(("parallel", "arbitrary")),
        name="peer_expert_in",
    )(h1b, ut, a_idx, b_idx)


def _peer_w_kernel(act_ref, g_ref, a_ref, b_ref, w_ref, wt_ref, *w3_refs):
    tb = act_ref.shape[0]
    act = act_ref[...]
    gelu = 0.5 * act * (1.0 + lax.erf(act * (2.0 ** -0.5)))
    wt_ref[...] = g_ref[...] * gelu
    sub = lax.broadcasted_iota(jnp.int32, (PEER_N_KEYS, PEER_SLOTS), 0)

    for g0 in range(0, tb, W_TOKEN_GROUP):
        w3_ref = w3_refs[(g0 // W_TOKEN_GROUP) % len(w3_refs)]
        for j in range(W_TOKEN_GROUP):
            t = g0 + j
            pt = jnp.where(sub == a_ref[t:t + 1, :], wt_ref[t:t + 1, :], 0.0).astype(BF16)
            qt = jnp.where(sub == b_ref[t:t + 1, :], 1.0, 0.0).astype(BF16)
            w3_ref[j * W_ROW_PITCH:j * W_ROW_PITCH + PEER_N_KEYS, :] = _dot_nt(pt, qt)
        for i1 in range(PEER_N_KEYS):
            rows = w3_ref[pl.ds(i1, W_TOKEN_GROUP, stride=W_ROW_PITCH), :]
            w_ref[g0:g0 + W_TOKEN_GROUP, i1 * PEER_N_KEYS:(i1 + 1) * PEER_N_KEYS] = rows.astype(BF16)


def _peer_w_call(act, gate, a_idx, b_idx, tb):
    n = act.shape[0]
    spec = pl.BlockSpec((tb, PEER_SLOTS), lambda i: (i, 0))
    return pl.pallas_call(
        _peer_w_kernel,
        grid=(n // tb,),
        in_specs=[spec, spec, spec, spec],
        out_specs=pl.BlockSpec((tb, PEER_N_EXPERTS), lambda i: (i, 0)),
        out_shape=jax.ShapeDtypeStruct((n, PEER_N_EXPERTS), BF16),
        scratch_shapes=[pltpu.VMEM((tb, PEER_SLOTS), F32)]
        + [pltpu.VMEM((W_TOKEN_GROUP * W_ROW_PITCH, PEER_N_KEYS), F32)] * W_GROUP_BUFFERS,
        compiler_params=_params(("parallel",)),
        name="peer_expert_weights",
    )(act, gate, a_idx, b_idx)


def _peer_out_kernel(w_ref, v_ref, h1_ref, h1b_ref, p_ref, gw_ref, pw_ref, g2_ref, b2_ref,
                     h2_ref, h2b_ref, acc_ref):
    j = pl.program_id(1)

    @pl.when(j == 0)
    def _():
        acc_ref[...] = jnp.zeros_like(acc_ref)

    acc_ref[...] += _dot(w_ref[...], v_ref[...])

    @pl.when(j == pl.num_programs(1) - 1)
    def _():
        gate = jax.nn.sigmoid(_dot(h1b_ref[...], gw_ref[...]))
        ple = gate * _dot(p_ref[...].astype(BF16), pw_ref[...])
        y = DEEPNORM_ALPHA * h1_ref[...] + acc_ref[...] + ple
        h2 = _layer_norm_rows(y, g2_ref[...], b2_ref[...])
        h2_ref[...] = h2
        h2b_ref[...] = h2.astype(BF16)


def _peer_out_call(w, v, h1, h1b, p, gw, pw, g2, b2, tm, tk):
    n, d = h1.shape
    row = lambda a: a.reshape(1, -1)
    return pl.pallas_call(
        _peer_out_kernel,
        grid=(n // tm, PEER_N_EXPERTS // tk),
        in_specs=[pl.BlockSpec((tm, tk), lambda i, j: (i, j)),
                  pl.BlockSpec((tk, d), lambda i, j: (j, 0)),
                  pl.BlockSpec((tm, d), lambda i, j: (i, 0)),
                  pl.BlockSpec((tm, d), lambda i, j: (i, 0)),
                  pl.BlockSpec((tm, PLE_DIM), lambda i, j: (i, 0)),
                  pl.BlockSpec(gw.shape, lambda i, j: (0, 0)),
                  pl.BlockSpec(pw.shape, lambda i, j: (0, 0)),
                  pl.BlockSpec((1, d), lambda i, j: (0, 0)),
                  pl.BlockSpec((1, d), lambda i, j: (0, 0))],
        out_specs=[pl.BlockSpec((tm, d), lambda i, j: (i, 0)),
                   pl.BlockSpec((tm, d), lambda i, j: (i, 0))],
        out_shape=[jax.ShapeDtypeStruct((n, d), F32), jax.ShapeDtypeStruct((n, d), BF16)],
        scratch_shapes=[pltpu.VMEM((tm, d), F32)],
        compiler_params=_params(("parallel", "arbitrary")),
        name="peer_out_ple_ln2",
    )(w, v, h1, h1b, p, gw, pw, row(g2), row(b2))


def _tiles(n, seq):
    pick = lambda pref, total: pref if total % pref == 0 else total
    return dict(
        rows=pick(512, n),
        tq=pick(512, seq), wide=pick(1024, seq),
        act_tb=pick(1024, n), act_ec=4096,
        w_tb=pick(128, n),
        out_tm=pick(512, n), out_tk=4096,
    )


def kernel(x, p, emb_ln_g, emb_ln_b, w_in, fox_fb, fox_norm_g, hg_lb_logits, hg_norm_g, w_out,
           ln1_g, ln1_b, peer_wq, peer_keys, peer_u, peer_v, ple_gate_w, ple_w, ln2_g, ln2_b):
    batch, seq, d = x.shape
    n = batch * seq
    assert n % SLAB_TOKENS == 0, "token count must be a multiple of the retrieval block"
    tl = _tiles(n, seq)
    h, hb = _ln_call(x.reshape(n, d), emb_ln_g, emb_ln_b, tl["rows"])
    scale = FOX_HEAD_DIM ** -0.5 * LOG2E
    o_ff = 3 * FOX_WIDTH
    o_hg = o_ff + FOX_HEADS
    for i in range(DEPTH):
        wi = w_in[i]
        wqkv = jnp.concatenate([wi[:, :FOX_WIDTH] * scale, wi[:, FOX_WIDTH:o_ff]], axis=1).astype(BF16)
        wff = jnp.pad(wi[:, o_ff:o_hg], ((0, 0), (0, LANES - FOX_HEADS))).astype(BF16)
        whg = wi[:, o_hg:].astype(BF16)
        qkv, ff, hgp = _inproj_call(hb, wqkv, wff, whg, tl["rows"])
        ff_t = ff[:, :FOX_HEADS].reshape(batch, seq, FOX_HEADS).transpose(0, 2, 1).reshape(batch * FOX_HEADS, seq)
        fb_rows = jnp.tile(fox_fb[i], batch).reshape(batch * FOX_HEADS, 1)
        c = _fox_cumsum_call(ff_t, fb_rows).reshape(batch, FOX_HEADS // 2, 2, seq)
        fox_o = _fox_call(qkv.reshape(batch, seq, 3 * FOX_WIDTH), c, batch, seq, tl["tq"], tl["wide"], FOX_PAIRS_PER_STEP)
        lbl = jnp.concatenate([hg_lb_logits.astype(F32), jnp.full((1, HG_WIDTH), float(i), F32)], axis=0)
        ohg = _hgrn_call(hgp.reshape(batch, seq, 4 * HG_WIDTH), lbl, batch, seq)
        h1, h1b = _mix_call(fox_o.reshape(n, FOX_WIDTH), ohg.reshape(n, HG_WIDTH), hgp, h,
                            fox_norm_g[i], hg_norm_g[i], w_out[i].astype(BF16), ln1_g[i], ln1_b[i], tl["rows"])
        keys = peer_keys[i].reshape(PEER_HEADS * 2, PEER_N_KEYS, PEER_HALF_DIM).astype(BF16)
        a_idx, b_idx, gate = _peerq_call(h1b, peer_wq[i].astype(BF16), keys)
        ut = peer_u[i].astype(BF16).T
        act = _peer_act_call(h1b, ut, a_idx, b_idx, tl["act_tb"], tl["act_ec"])
        w = _peer_w_call(act, gate, a_idx, b_idx, tl["w_tb"])
        h, hb = _peer_out_call(w, peer_v[i].astype(BF16), h1, h1b, p[i].reshape(n, PLE_DIM),
                               ple_gate_w[i].astype(BF16), ple_w[i].astype(BF16), ln2_g[i], ln2_b[i],
                               tl["out_tm"], tl["out_tk"])
    return h.reshape(batch, seq, d)
```

```python
import functools

import jax
import jax.numpy as jnp
from jax import lax
from jax.experimental import pallas as pl
from jax.experimental.pallas import tpu as pltpu

D_MODEL = 1024
DEPTH = 2
PLE_DIM = 256
FOX_HEADS = 8
FOX_HEAD_DIM = 64
FOX_WIDTH = FOX_HEADS * FOX_HEAD_DIM
HG_HEADS = 4
HG_KEY_DIM = 128
HG_VAL_DIM = 128
HG_WIDTH = HG_HEADS * HG_VAL_DIM
PEER_HEADS = 8
PEER_N_KEYS = 128
PEER_N_EXPERTS = PEER_N_KEYS * PEER_N_KEYS
PEER_HALF_DIM = 128
PEER_TOPK = 16
PEER_SLOTS = PEER_HEADS * PEER_TOPK
DEEPNORM_ALPHA = (2 * DEPTH) ** 0.25
LN_EPS = 1e-5
RMS_EPS = 1e-6
MASK_VALUE = -1e30
LOG2E = 1.4426950408889634
LB_FLOOR = 1e-30

LANES = 128
SUBLANES = 8
VMEM_LIMIT_BYTES = 56 * 1024 * 1024

FOX_PAIRS_PER_STEP = 1
HG_CHUNK = 16
HG_CUMSUM_ROWS = 128
W_ROW_PITCH = PEER_N_KEYS + SUBLANES
W_TOKEN_GROUP = 16

F32 = jnp.float32
BF16 = jnp.bfloat16


def _params(sem):
    return pltpu.CompilerParams(dimension_semantics=sem, vmem_limit_bytes=VMEM_LIMIT_BYTES)


def _dot(a, b):
    return jnp.dot(a, b, preferred_element_type=F32)


def _dot_nt(a, b):
    return lax.dot_general(a, b, (((1,), (1,)), ((), ())), preferred_element_type=F32)


def _layer_norm_rows(y, g, b):
    mu = jnp.mean(y, axis=-1, keepdims=True)
    d = y - mu
    var = jnp.mean(d * d, axis=-1, keepdims=True)
    return d * lax.rsqrt(var + LN_EPS) * g + b


def _ln_kernel(x_ref, g_ref, b_ref, h_ref, hb_ref):
    h = _layer_norm_rows(x_ref[...], g_ref[...], b_ref[...])
    h_ref[...] = h
    hb_ref[...] = h.astype(BF16)


def _ln_call(x, g, b, tm):
    n, d = x.shape
    return pl.pallas_call(
        _ln_kernel,
        grid=(n // tm,),
        in_specs=[pl.BlockSpec((tm, d), lambda i: (i, 0)),
                  pl.BlockSpec((1, d), lambda i: (0, 0)),
                  pl.BlockSpec((1, d), lambda i: (0, 0))],
        out_specs=[pl.BlockSpec((tm, d), lambda i: (i, 0)),
                   pl.BlockSpec((tm, d), lambda i: (i, 0))],
        out_shape=[jax.ShapeDtypeStruct((n, d), F32), jax.ShapeDtypeStruct((n, d), BF16)],
        compiler_params=_params(("parallel",)),
        name="emb_ln",
    )(x, g.reshape(1, d), b.reshape(1, d))


def _inproj_kernel(h_ref, wqkv_ref, wff_ref, whg_ref, qkv_ref, ff_ref, hg_ref):
    h = h_ref[...]
    qkv_ref[...] = _dot(h, wqkv_ref[...]).astype(BF16)
    ff_ref[...] = _dot(h, wff_ref[...])
    hg_ref[...] = _dot(h, whg_ref[...])


def _inproj_call(hb, wqkv, wff, whg, tm):
    n, d = hb.shape
    return pl.pallas_call(
        _inproj_kernel,
        grid=(n // tm,),
        in_specs=[pl.BlockSpec((tm, d), lambda i: (i, 0)),
                  pl.BlockSpec(wqkv.shape, lambda i: (0, 0)),
                  pl.BlockSpec(wff.shape, lambda i: (0, 0)),
                  pl.BlockSpec(whg.shape, lambda i: (0, 0))],
        out_specs=[pl.BlockSpec((tm, wqkv.shape[1]), lambda i: (i, 0)),
                   pl.BlockSpec((tm, wff.shape[1]), lambda i: (i, 0)),
                   pl.BlockSpec((tm, whg.shape[1]), lambda i: (i, 0))],
        out_shape=[jax.ShapeDtypeStruct((n, wqkv.shape[1]), BF16),
                   jax.ShapeDtypeStruct((n, wff.shape[1]), F32),
                   jax.ShapeDtypeStruct((n, whg.shape[1]), F32)],
        compiler_params=_params(("parallel",)),
        name="in_proj",
    )(hb, wqkv, wff, whg)


def _log_sigmoid(z):
    return jnp.minimum(z, 0.0) - jnp.log(1.0 + jnp.exp(-jnp.abs(z)))


def _fox_cumsum_kernel(ff_ref, fb_ref, c_ref):
    rows, t = ff_ref.shape
    lf = _log_sigmoid(ff_ref[...] + fb_ref[...]) * LOG2E
    r = lax.broadcasted_iota(jnp.int32, (LANES, LANES), 0)
    c = lax.broadcasted_iota(jnp.int32, (LANES, LANES), 1)
    upper = (r <= c).astype(F32)
    carry = jnp.zeros((rows, 1), F32)
    for j in range(t // LANES):
        blk = lf[:, j * LANES:(j + 1) * LANES]
        cs = jnp.dot(blk, upper, preferred_element_type=F32, precision=lax.Precision.HIGHEST)
        cs = cs + carry
        c_ref[:, j * LANES:(j + 1) * LANES] = cs
        carry = cs[:, LANES - 1:LANES]


def _fox_cumsum_call(ff_t, fb_rows):
    rows, t = ff_t.shape
    tr = SUBLANES
    return pl.pallas_call(
        _fox_cumsum_kernel,
        grid=(rows // tr,),
        in_specs=[pl.BlockSpec((tr, t), lambda i: (i, 0)),
                  pl.BlockSpec((tr, 1), lambda i: (i, 0))],
        out_specs=pl.BlockSpec((tr, t), lambda i: (i, 0)),
        out_shape=jax.ShapeDtypeStruct((rows, t), F32),
        compiler_params=_params(("parallel",)),
        name="fox_cumsum",
    )(ff_t, fb_rows)


def _split3(x):
    hi = x.astype(BF16).astype(F32)
    r1 = x - hi
    mid = r1.astype(BF16).astype(F32)
    lo = (r1 - mid).astype(BF16).astype(F32)
    return hi, mid, lo


def _fox_kernel(q_ref, k_ref, v_ref, c_ref, o_ref, kx_ref, vx_ref, qe_ref, *, tq, wide, seq, pairs):
    qi = pl.program_id(2)
    lane = lax.broadcasted_iota(jnp.int32, (1, LANES), 1)
    data = [lane < FOX_HEAD_DIM, lane >= FOX_HEAD_DIM]
    spare = [FOX_HEAD_DIM, 0]
    heads = [(p, h) for p in range(pairs) for h in range(2)]
    slab = lambda p: slice(p * LANES, (p + 1) * LANES)

    sub8 = lax.broadcasted_iota(jnp.int32, (SUBLANES, LANES), 0)

    def bias_tile(h, pieces):
        rows8 = jnp.zeros((SUBLANES, LANES), F32)
        for j, piece in enumerate(pieces):
            rows8 = jnp.where(sub8 == j, piece, rows8)
        blocks = [jnp.zeros((spare[h], LANES), F32)] if spare[h] else []
        blocks += [rows8, jnp.zeros((LANES - spare[h] - SUBLANES, LANES), F32)]
        return jnp.transpose(jnp.concatenate(blocks, axis=0))

    @pl.when(qi == 0)
    def _():
        for n, (p, h) in enumerate(heads):
            for j in range(seq // LANES):
                rows = slice(j * LANES, (j + 1) * LANES)
                hi, mid, lo = _split3(c_ref[p, h:h + 1, rows])
                one = jnp.ones_like(hi)
                k_extra = bias_tile(h, (-hi, -mid, -lo, one, one, one))
                kx_ref[n, rows, :] = jnp.where(data[h], k_ref[rows, slab(p)].astype(F32), k_extra).astype(BF16)
                qe_ref[n, rows, :] = bias_tile(h, (one, one, one, hi, mid, lo)).astype(BF16)
                vx_ref[n, rows, :] = jnp.where(data[h], v_ref[rows, slab(p)].astype(F32), 1.0).astype(BF16)

    q0 = pl.multiple_of(qi * tq, tq)
    qx = []
    for n, (p, h) in enumerate(heads):
        q = q_ref[:, slab(p)].astype(F32)
        q_extra = qe_ref[n, pl.ds(q0, tq), :].astype(F32)
        qx.append((jnp.where(data[h], q, 0.0) + q_extra).astype(BF16))

    def block(k0, width, masked, carry):
        out = []
        for n in range(len(heads)):
            m_old, acc_old = carry[n]
            s = _dot_nt(qx[n], kx_ref[n, pl.ds(k0, width), :])
            if masked:
                row = lax.broadcasted_iota(jnp.int32, (tq, width), 0) + q0
                col = lax.broadcasted_iota(jnp.int32, (tq, width), 1) + k0
                s = jnp.where(col <= row, s, MASK_VALUE)
            m_new = jnp.maximum(m_old, jnp.max(s, axis=-1, keepdims=True))
            alpha = jnp.exp2(m_old - m_new)
            p = jnp.exp2(s - m_new).astype(BF16)
            acc_new = alpha * acc_old + _dot(p, vx_ref[n, pl.ds(k0, width), :])
            out.append((m_new, acc_new))
        return tuple(out)

    init = tuple((jnp.full((tq, 1), MASK_VALUE, F32), jnp.zeros((tq, LANES), F32)) for _ in heads)
    assert wide in (tq, 2 * tq)
    n_wide = q0 // wide
    carry = lax.fori_loop(
        0, n_wide, lambda j, c: block(pl.multiple_of(j * wide, wide), wide, False, c), init)
    if wide == tq:
        carry = block(q0, tq, True, carry)
    else:
        carry = lax.cond(q0 % wide == 0,
                         lambda c: block(q0, tq, True, c),
                         lambda c: block(pl.multiple_of(n_wide * wide, wide), wide, True, c),
                         carry)
    for p in range(pairs):
        res = []
        for h in range(2):
            acc = carry[2 * p + h][1]
            res.append(acc / pltpu.roll(acc, FOX_HEAD_DIM, axis=1))
        o_ref[:, slab(p)] = jnp.where(data[0], res[0], res[1])


def _fox_call(qkv, c, batch, seq, tq, wide, pairs):
    steps = FOX_HEADS // 2 // pairs
    width = pairs * LANES
    return pl.pallas_call(
        functools.partial(_fox_kernel, tq=tq, wide=wide, seq=seq, pairs=pairs),
        grid=(batch, steps, seq // tq),
        in_specs=[pl.BlockSpec((None, tq, width), lambda b, p, i: (b, i, p)),
                  pl.BlockSpec((None, seq, width), lambda b, p, i: (b, 0, steps + p)),
                  pl.BlockSpec((None, seq, width), lambda b, p, i: (b, 0, 2 * steps + p)),
                  pl.BlockSpec((None, pairs, 2, seq), lambda b, p, i: (b, p, 0, 0))],
        out_specs=pl.BlockSpec((None, tq, width), lambda b, p, i: (b, i, p)),
        out_shape=jax.ShapeDtypeStruct((batch, seq, FOX_WIDTH), F32),
        scratch_shapes=[pltpu.VMEM((2 * pairs, seq, LANES), BF16),
                        pltpu.VMEM((2 * pairs, seq, LANES), BF16),
                        pltpu.VMEM((2 * pairs, seq, LANES), BF16)],
        compiler_params=_params(("parallel", "parallel", "arbitrary")),
        name="fox_attention",
    )(qkv, qkv, qkv, c)


def _hgrn_kernel(q_ref, z_ref, v_ref, lbl_ref, o_ref, lf_ref, kk_ref, st_ref, *, seq):
    logits = lbl_ref[...]
    lg = logits[:DEPTH, :]
    mx = jnp.max(lg, axis=0, keepdims=True)
    ex = jnp.exp(lg - mx)
    probs = ex / jnp.sum(ex, axis=0, keepdims=True)
    sel = logits[DEPTH:DEPTH + 1, :]
    lb = jnp.zeros_like(sel)
    run = jnp.zeros_like(sel)
    for i in range(DEPTH):
        run = run + probs[i:i + 1, :]
        lb = jnp.where(sel == float(i), run - probs[0:1, :], lb)
    log_lb = jnp.log(jnp.maximum(lb, LB_FLOOR))
    log_1m = jnp.log(1.0 - lb)

    rr = lax.broadcasted_iota(jnp.int32, (HG_CUMSUM_ROWS, HG_CUMSUM_ROWS), 0)
    cc = lax.broadcasted_iota(jnp.int32, (HG_CUMSUM_ROWS, HG_CUMSUM_ROWS), 1)
    tri = ((rr // HG_CHUNK == cc // HG_CHUNK) & (cc <= rr)).astype(F32).astype(BF16)

    def prep(j, _):
        r0 = pl.multiple_of(j * HG_CUMSUM_ROWS, HG_CUMSUM_ROWS)
        z = z_ref[pl.ds(r0, HG_CUMSUM_ROWS), :]
        a = log_1m + _log_sigmoid(z)
        mm = jnp.maximum(log_lb, a)
        lf = mm + jnp.log(1.0 + jnp.exp(-jnp.abs(log_lb - a)))
        lf_ref[pl.ds(r0, HG_CUMSUM_ROWS), :] = sum(_dot(tri, piece.astype(BF16)) for piece in _split3(lf))
        kk_ref[pl.ds(r0, HG_CUMSUM_ROWS), :] = (1.0 - lb) * jax.nn.sigmoid(-z)
        return 0

    lax.fori_loop(0, seq // HG_CUMSUM_ROWS, prep, 0, unroll=2)
    st_ref[...] = jnp.zeros_like(st_ref)
    row8 = lax.broadcasted_iota(jnp.int32, (SUBLANES, 1), 0)

    def chunk(ci, _):
        r0 = pl.multiple_of(ci * HG_CHUNK, HG_CHUNK)
        b = lf_ref[pl.ds(r0, HG_CHUNK), :]
        q = q_ref[pl.ds(r0, HG_CHUNK), :]
        kk = kk_ref[pl.ds(r0, HG_CHUNK), :]
        v = v_ref[pl.ds(r0, HG_CHUNK), :]
        b_last = b[HG_CHUNK - 1:HG_CHUNK, :]
        qe = (q * jnp.exp(b)).astype(BF16)
        kd = (kk * jnp.exp(b_last - b)).astype(BF16)
        e_last = jnp.exp(b_last)
        outs = []
        for h in range(HG_HEADS):
            sl = slice(h * HG_KEY_DIM, (h + 1) * HG_KEY_DIM)
            st = st_ref[h]
            o = _dot_nt(qe[:, sl], st.astype(BF16))
            bh, qh, kh, vh = b[:, sl], q[:, sl], kk[:, sl], v[:, sl]
            halves = []
            for half in range(HG_CHUNK // SUBLANES):
                rows = slice(half * SUBLANES, (half + 1) * SUBLANES)
                bt, qt, acc = bh[rows], qh[rows], o[rows]
                for s in range((half + 1) * SUBLANES):
                    diff = bt - bh[s:s + 1, :]
                    first_live = s - half * SUBLANES
                    if first_live > 0:
                        diff = jnp.where(row8 >= first_live, diff, MASK_VALUE)
                    a_col = jnp.sum(qt * kh[s:s + 1, :] * jnp.exp(diff), axis=-1, keepdims=True)
                    acc = acc + a_col * vh[s:s + 1, :]
                halves.append(acc)
            o = jnp.concatenate(halves, axis=0)
            upd = lax.dot_general(vh.astype(BF16), kd[:, sl], (((0,), (0,)), ((), ())),
                                  preferred_element_type=F32)
            st_ref[h] = st * e_last[:, sl] + upd
            outs.append(o)
        o_ref[pl.ds(r0, HG_CHUNK), :] = jnp.concatenate(outs, axis=1)
        return 0

    lax.fori_loop(0, seq // HG_CHUNK, chunk, 0, unroll=4)


def _hgrn_call(hgp, lbl, batch, seq):
    return pl.pallas_call(
        functools.partial(_hgrn_kernel, seq=seq),
        grid=(batch,),
        in_specs=[pl.BlockSpec((None, seq, HG_WIDTH), lambda b: (b, 0, 0)),
                  pl.BlockSpec((None, seq, HG_WIDTH), lambda b: (b, 0, 1)),
                  pl.BlockSpec((None, seq, HG_WIDTH), lambda b: (b, 0, 2)),
                  pl.BlockSpec(lbl.shape, lambda b: (0, 0))],
        out_specs=pl.BlockSpec((None, seq, HG_WIDTH), lambda b: (b, 0, 0)),
        out_shape=jax.ShapeDtypeStruct((batch, seq, HG_WIDTH), F32),
        scratch_shapes=[pltpu.VMEM((seq, HG_WIDTH), F32),
                        pltpu.VMEM((seq, HG_WIDTH), F32),
                        pltpu.VMEM((HG_HEADS, HG_VAL_DIM, HG_KEY_DIM), F32)],
        compiler_params=_params(("parallel",)),
        name="hgrn2",
    )(hgp, hgp, hgp, lbl)


def _mix_kernel(fox_ref, ohg_ref, gate_ref, h_ref, gf_ref, gh_ref, wo_ref, g1_ref, b1_ref,
                h1_ref, h1b_ref):
    fox = fox_ref[...]
    fox = fox * lax.rsqrt(jnp.mean(fox * fox, axis=-1, keepdims=True) + RMS_EPS) * gf_ref[...]
    ohg = ohg_ref[...]
    gh = gh_ref[...]
    gate = gate_ref[...]
    parts = [fox.astype(BF16)]
    for h in range(HG_HEADS):
        sl = slice(h * HG_VAL_DIM, (h + 1) * HG_VAL_DIM)
        o = ohg[:, sl]
        o = o * lax.rsqrt(jnp.mean(o * o, axis=-1, keepdims=True) + RMS_EPS) * gh[:, sl]
        g = gate[:, sl]
        parts.append((o * (g * jax.nn.sigmoid(g))).astype(BF16))
    cat = jnp.concatenate(parts, axis=1)
    mix = _dot(cat, wo_ref[...])
    h1 = _layer_norm_rows(DEEPNORM_ALPHA * h_ref[...] + mix, g1_ref[...], b1_ref[...])
    h1_ref[...] = h1
    h1b_ref[...] = h1.astype(BF16)


def _mix_call(fox_o, ohg, hgp, h, gf, gh, wo, g1, b1, tm):
    n, d = h.shape
    row = lambda a: a.reshape(1, -1)
    return pl.pallas_call(
        _mix_kernel,
        grid=(n // tm,),
        in_specs=[pl.BlockSpec((tm, FOX_WIDTH), lambda i: (i, 0)),
                  pl.BlockSpec((tm, HG_WIDTH), lambda i: (i, 0)),
                  pl.BlockSpec((tm, HG_WIDTH), lambda i: (i, 3)),
                  pl.BlockSpec((tm, d), lambda i: (i, 0)),
                  pl.BlockSpec((1, FOX_WIDTH), lambda i: (0, 0)),
                  pl.BlockSpec((1, HG_WIDTH), lambda i: (0, 0)),
                  pl.BlockSpec(wo.shape, lambda i: (0, 0)),
                  pl.BlockSpec((1, d), lambda i: (0, 0)),
                  pl.BlockSpec((1, d), lambda i: (0, 0))],
        out_specs=[pl.BlockSpec((tm, d), lambda i: (i, 0)),
                   pl.BlockSpec((tm, d), lambda i: (i, 0))],
        out_shape=[jax.ShapeDtypeStruct((n, d), F32), jax.ShapeDtypeStruct((n, d), BF16)],
        compiler_params=_params(("parallel",)),
        name="mix_out_ln1",
    )(fox_o, ohg, hgp, h, row(gf), row(gh), wo, row(g1), row(b1))


class _Network:
    def __init__(self):
        self.comparators = []
        self.outputs = []


def _net_merge(net, xs, ys):
    if not xs:
        return list(ys)
    if not ys:
        return list(xs)
    if len(xs) == 1 and len(ys) == 1:
        net.comparators.append((xs[0], ys[0]))
        return [xs[0], ys[0]]
    evens = _net_merge(net, xs[0::2], ys[0::2])
    odds = _net_merge(net, xs[1::2], ys[1::2])
    out = [evens[0]]
    for i, w in enumerate(odds):
        if i + 1 < len(evens):
            net.comparators.append((evens[i + 1], w))
            out += [evens[i + 1], w]
        else:
            out.append(w)
    return out + evens[len(odds) + 1:]


def _net_sort(net, ws):
    if len(ws) <= 1:
        return list(ws)
    mid = len(ws) // 2
    return _net_merge(net, _net_sort(net, ws[:mid]), _net_sort(net, ws[mid:]))


def _top_network(sizes, presorted, keep):
    net = _Network()
    lists, w = [], 0
    for s in sizes:
        ws = list(range(w, w + s))
        w += s
        lists.append(ws if presorted else _net_sort(net, ws))
    while len(lists) > 1:
        lists.sort(key=len)
        lists = [_net_merge(net, lists[0], lists[1])[:keep]] + lists[2:]
    net.outputs = lists[0][:keep]
    needed, kept = set(net.outputs), []
    for a, b in reversed(net.comparators):
        if a in needed or b in needed:
            kept.append((a, b, a in needed, b in needed))
            needed.update((a, b))
    net.comparators = kept[::-1]
    return net


def _run_network(net, wires):
    wires = list(wires)
    for a, b, need_a, need_b in net.comparators:
        (va, ia), (vb, ib) = wires[a], wires[b]
        a_first = (va > vb) | ((va == vb) & (ia < ib))
        if need_a:
            wires[a] = (jnp.where(a_first, va, vb), jnp.where(a_first, ia, ib))
        if need_b:
            wires[b] = (jnp.where(a_first, vb, va), jnp.where(a_first, ib, ia))
    return [wires[w] for w in net.outputs]


def _pair_lists():
    ok = lambda j1, j2: (j1 + 1) * (j2 + 1) <= PEER_TOPK
    lists = []
    for d in range(PEER_TOPK):
        row = [(d, j2) for j2 in range(d, PEER_TOPK) if ok(d, j2)]
        col = [(j1, d) for j1 in range(d + 1, PEER_TOPK) if ok(j1, d)]
        lists += [l for l in (row, col) if l]
    return lists


FLAT_SHIFT = 14
_KEY_NET = _top_network([PEER_TOPK] * (PEER_N_KEYS // PEER_TOPK), presorted=False, keep=PEER_TOPK)
_PAIR_LISTS = _pair_lists()
_PAIR_NET = _top_network([len(l) for l in _PAIR_LISTS], presorted=True, keep=PEER_TOPK)


SLAB_TOKENS = SUBLANES * LANES
SLAB_ROWS = SUBLANES * W_ROW_PITCH


def _peerq_kernel(h_ref, wq_ref, keys_ref, a_ref, b_ref, g_ref,
                  sc_ref, tv_ref, ti_ref, oa_ref, ob_ref, og_ref):
    n_lists = 2 * PEER_HEADS
    q = _dot(h_ref[...], wq_ref[...]).astype(BF16)
    for hp in range(n_lists):
        s = _dot_nt(keys_ref[hp], q[:, hp * PEER_HALF_DIM:(hp + 1) * PEER_HALF_DIM])
        for g in range(SUBLANES):
            sc_ref[pl.ds(hp * SLAB_ROWS + g * W_ROW_PITCH, PEER_N_KEYS), :] = s[:, g * LANES:(g + 1) * LANES]

    def keys_top(hp, _):
        base = hp * SLAB_ROWS
        wires = [(sc_ref[pl.ds(base + n, SUBLANES, stride=W_ROW_PITCH), :],
                  jnp.full((SUBLANES, LANES), n, jnp.int32)) for n in range(PEER_N_KEYS)]
        for j, (v, i) in enumerate(_run_network(_KEY_NET, wires)):
            tv_ref[hp * PEER_TOPK + j] = v
            ti_ref[hp * PEER_TOPK + j] = i
        return 0

    lax.fori_loop(0, n_lists, keys_top, 0)

    def pairs_top(h, _):
        r1 = 2 * h * PEER_TOPK
        r2 = r1 + PEER_TOPK
        s1 = [tv_ref[r1 + j] for j in range(PEER_TOPK)]
        s2 = [tv_ref[r2 + j] for j in range(PEER_TOPK)]
        e1 = [ti_ref[r1 + j] * PEER_N_KEYS for j in range(PEER_TOPK)]
        i2 = [ti_ref[r2 + j] for j in range(PEER_TOPK)]
        wires = [(s1[j1] + s2[j2], e1[j1] + i2[j2] + ((j1 * PEER_TOPK + j2) << FLAT_SHIFT))
                 for lst in _PAIR_LISTS for (j1, j2) in lst]
        top = _run_network(_PAIR_NET, wires)
        es = [jnp.exp(v - top[0][0]) for v, _ in top]
        total = es[0]
        for e in es[1:]:
            total = total + e
        inv = 1.0 / total
        for k, ((_, key), e) in enumerate(zip(top, es)):
            expert = key & ((1 << FLAT_SHIFT) - 1)
            rows = pl.ds(h * PEER_TOPK + k, SUBLANES, stride=W_ROW_PITCH)
            oa_ref[rows, :] = lax.shift_right_logical(expert, 7)
            ob_ref[rows, :] = expert & (PEER_N_KEYS - 1)
            og_ref[rows, :] = e * inv
        return 0

    lax.fori_loop(0, PEER_HEADS, pairs_top, 0)
    for g in range(SUBLANES):
        rows = pl.ds(g * W_ROW_PITCH, PEER_SLOTS)
        toks = slice(g * LANES, (g + 1) * LANES)
        a_ref[toks, :] = jnp.transpose(oa_ref[rows, :])
        b_ref[toks, :] = jnp.transpose(ob_ref[rows, :])
        g_ref[toks, :] = jnp.transpose(og_ref[rows, :])


def _peerq_call(h1b, wq, keys):
    n, d = h1b.shape
    tb = SLAB_TOKENS
    slot = lambda dt: jax.ShapeDtypeStruct((n, PEER_SLOTS), dt)
    return pl.pallas_call(
        _peerq_kernel,
        grid=(n // tb,),
        in_specs=[pl.BlockSpec((tb, d), lambda i: (i, 0)),
                  pl.BlockSpec(wq.shape, lambda i: (0, 0)),
                  pl.BlockSpec(keys.shape, lambda i: (0, 0, 0))],
        out_specs=[pl.BlockSpec((tb, PEER_SLOTS), lambda i: (i, 0))] * 3,
        out_shape=[slot(jnp.int32), slot(jnp.int32), slot(F32)],
        scratch_shapes=[pltpu.VMEM((2 * PEER_HEADS * SLAB_ROWS, LANES), F32),
                        pltpu.VMEM((2 * PEER_HEADS * PEER_TOPK, SUBLANES, LANES), F32),
                        pltpu.VMEM((2 * PEER_HEADS * PEER_TOPK, SUBLANES, LANES), jnp.int32),
                        pltpu.VMEM((SLAB_ROWS, LANES), jnp.int32),
                        pltpu.VMEM((SLAB_ROWS, LANES), jnp.int32),
                        pltpu.VMEM((SLAB_ROWS, LANES), F32)],
        compiler_params=_params(("parallel",)),
        name="peer_retrieve",
    )(h1b, wq, keys)


def _peer_act_kernel(x_ref, ut_ref, a_ref, b_ref, o_ref, *, ec):
    j = pl.program_id(1)

    @pl.when(j == 0)
    def _():
        o_ref[...] = jnp.zeros_like(o_ref)

    act = _dot(x_ref[...], ut_ref[...])
    a_idx = a_ref[...]
    b_idx = b_ref[...]
    acc = o_ref[...]
    groups = ec // PEER_N_KEYS
    for g in range(groups):
        i1 = j * groups + g
        picked = jnp.take_along_axis(act[:, g * PEER_N_KEYS:(g + 1) * PEER_N_KEYS], b_idx, axis=1)
        acc = jnp.where(a_idx == i1, picked, acc)
    o_ref[...] = acc


def _peer_act_call(h1b, ut, a_idx, b_idx, tb, ec):
    n, d = h1b.shape
    return pl.pallas_call(
        functools.partial(_peer_act_kernel, ec=ec),
        grid=(n // tb, PEER_N_EXPERTS // ec),
        in_specs=[pl.BlockSpec((tb, d), lambda i, j: (i, 0)),
                  pl.BlockSpec((d, ec), lambda i, j: (0, j)),
                  pl.BlockSpec((tb, PEER_SLOTS), lambda i, j: (i, 0)),
                  pl.BlockSpec((tb, PEER_SLOTS), lambda i, j: (i, 0))],
        out_specs=pl.BlockSpec((tb, PEER_SLOTS), lambda i, j: (i, 0)),
        out_shape=jax.ShapeDtypeStruct((n, PEER_SLOTS), F32),
        compiler_params=_params(("parallel", "arbitrary")),
        name="peer_expert_in",
    )(h1b, ut, a_idx, b_idx)


def _peer_w_kernel(act_ref, g_ref, a_ref, b_ref, w_ref, wt_ref, w3_ref):
    tb = act_ref.shape[0]
    act = act_ref[...]
    gelu = 0.5 * act * (1.0 + lax.erf(act * (2.0 ** -0.5)))
    wt_ref[...] = g_ref[...] * gelu
    sub = lax.broadcasted_iota(jnp.int32, (PEER_N_KEYS, PEER_SLOTS), 0)

    def build_pair(w3_ref, g0, j):
        pts, qts = [], []
        for t in (g0 + j, g0 + j + 1):
            pts.append(jnp.where(sub == a_ref[t:t + 1, :], wt_ref[t:t + 1, :], 0.0).astype(BF16))
            qts.append(jnp.where(sub == b_ref[t:t + 1, :], 1.0, 0.0).astype(BF16))
        zero = jnp.zeros_like(qts[0])
        rhs_t = jnp.concatenate([jnp.concatenate([qts[0], zero], axis=1),
                                 jnp.concatenate([zero, qts[1]], axis=1)], axis=0)
        both = _dot_nt(jnp.concatenate(pts, axis=1), rhs_t)
        for k in range(2):
            r0 = (j + k) * W_ROW_PITCH
            w3_ref[r0:r0 + PEER_N_KEYS, :] = both[:, k * PEER_N_KEYS:(k + 1) * PEER_N_KEYS]

    def relayout(w3_ref, g0, i1):
        rows = w3_ref[pl.ds(i1, W_TOKEN_GROUP, stride=W_ROW_PITCH), :]
        w_ref[g0:g0 + W_TOKEN_GROUP, i1 * PEER_N_KEYS:(i1 + 1) * PEER_N_KEYS] = rows.astype(BF16)

    for g0 in range(0, tb, W_TOKEN_GROUP):
        for j in range(0, W_TOKEN_GROUP, 2):
            build_pair(w3_ref, g0, j)
        for i1 in range(PEER_N_KEYS):
            relayout(w3_ref, g0, i1)


def _peer_w_call(act, gate, a_idx, b_idx, tb):
    n = act.shape[0]
    spec = pl.BlockSpec((tb, PEER_SLOTS), lambda i: (i, 0))
    return pl.pallas_call(
        _peer_w_kernel,
        grid=(n // tb,),
        in_specs=[spec, spec, spec, spec],
        out_specs=pl.BlockSpec((tb, PEER_N_EXPERTS), lambda i: (i, 0)),
        out_shape=jax.ShapeDtypeStruct((n, PEER_N_EXPERTS), BF16),
        scratch_shapes=[pltpu.VMEM((tb, PEER_SLOTS), F32),
                        pltpu.VMEM((W_TOKEN_GROUP * W_ROW_PITCH, PEER_N_KEYS), F32)],
        compiler_params=_params(("parallel",)),
        name="peer_expert_weights",
    )(act, gate, a_idx, b_idx)


def _peer_out_kernel(w_ref, v_ref, h1_ref, h1b_ref, p_ref, gw_ref, pw_ref, g2_ref, b2_ref,
                     h2_ref, h2b_ref, acc_ref):
    j = pl.program_id(1)

    @pl.when(j == 0)
    def _():
        acc_ref[...] = jnp.zeros_like(acc_ref)

    acc_ref[...] += _dot(w_ref[...], v_ref[...])

    @pl.when(j == pl.num_programs(1) - 1)
    def _():
        gate = jax.nn.sigmoid(_dot(h1b_ref[...], gw_ref[...]))
        ple = gate * _dot(p_ref[...].astype(BF16), pw_ref[...])
        y = DEEPNORM_ALPHA * h1_ref[...] + acc_ref[...] + ple
        h2 = _layer_norm_rows(y, g2_ref[...], b2_ref[...])
        h2_ref[...] = h2
        h2b_ref[...] = h2.astype(BF16)


def _peer_out_call(w, v, h1, h1b, p, gw, pw, g2, b2, tm, tk):
    n, d = h1.shape
    row = lambda a: a.reshape(1, -1)
    return pl.pallas_call(
        _peer_out_kernel,
        grid=(n // tm, PEER_N_EXPERTS // tk),
        in_specs=[pl.BlockSpec((tm, tk), lambda i, j: (i, j)),
                  pl.BlockSpec((tk, d), lambda i, j: (j, 0)),
                  pl.BlockSpec((tm, d), lambda i, j: (i, 0)),
                  pl.BlockSpec((tm, d), lambda i, j: (i, 0)),
                  pl.BlockSpec((tm, PLE_DIM), lambda i, j: (i, 0)),
                  pl.BlockSpec(gw.shape, lambda i, j: (0, 0)),
                  pl.BlockSpec(pw.shape, lambda i, j: (0, 0)),
                  pl.BlockSpec((1, d), lambda i, j: (0, 0)),
                  pl.BlockSpec((1, d), lambda i, j: (0, 0))],
        out_specs=[pl.BlockSpec((tm, d), lambda i, j: (i, 0)),
                   pl.BlockSpec((tm, d), lambda i, j: (i, 0))],
        out_shape=[jax.ShapeDtypeStruct((n, d), F32), jax.ShapeDtypeStruct((n, d), BF16)],
        scratch_shapes=[pltpu.VMEM((tm, d), F32)],
        compiler_params=_params(("parallel", "arbitrary")),
        name="peer_out_ple_ln2",
    )(w, v, h1, h1b, p, gw, pw, row(g2), row(b2))


def _tiles(n, seq):
    pick = lambda pref, total: pref if total % pref == 0 else total
    return dict(
        rows=pick(512, n),
        tq=pick(512, seq), wide=pick(1024, seq),
        act_tb=pick(1024, n), act_ec=4096,
        w_tb=pick(128, n),
        out_tm=pick(512, n), out_tk=4096,
    )


def kernel(x, p, emb_ln_g, emb_ln_b, w_in, fox_fb, fox_norm_g, hg_lb_logits, hg_norm_g, w_out,
           ln1_g, ln1_b, peer_wq, peer_keys, peer_u, peer_v, ple_gate_w, ple_w, ln2_g, ln2_b):
    batch, seq, d = x.shape
    n = batch * seq
    assert n % SLAB_TOKENS == 0, "token count must be a multiple of the retrieval block"
    tl = _tiles(n, seq)
    h, hb = _ln_call(x.reshape(n, d), emb_ln_g, emb_ln_b, tl["rows"])
    scale = FOX_HEAD_DIM ** -0.5 * LOG2E
    o_ff = 3 * FOX_WIDTH
    o_hg = o_ff + FOX_HEADS
    for i in range(DEPTH):
        wi = w_in[i]
        wqkv = jnp.concatenate([wi[:, :FOX_WIDTH] * scale, wi[:, FOX_WIDTH:o_ff]], axis=1).astype(BF16)
        wff = jnp.pad(wi[:, o_ff:o_hg], ((0, 0), (0, LANES - FOX_HEADS))).astype(BF16)
        whg = wi[:, o_hg:].astype(BF16)
        qkv, ff, hgp = _inproj_call(hb, wqkv, wff, whg, tl["rows"])
        ff_t = ff[:, :FOX_HEADS].reshape(batch, seq, FOX_HEADS).transpose(0, 2, 1).reshape(batch * FOX_HEADS, seq)
        fb_rows = jnp.tile(fox_fb[i], batch).reshape(batch * FOX_HEADS, 1)
        c = _fox_cumsum_call(ff_t, fb_rows).reshape(batch, FOX_HEADS // 2, 2, seq)
        fox_o = _fox_call(qkv.reshape(batch, seq, 3 * FOX_WIDTH), c, batch, seq, tl["tq"], tl["wide"], FOX_PAIRS_PER_STEP)
        lbl = jnp.concatenate([hg_lb_logits.astype(F32), jnp.full((1, HG_WIDTH), float(i), F32)], axis=0)
        ohg = _hgrn_call(hgp.reshape(batch, seq, 4 * HG_WIDTH), lbl, batch, seq)
        h1, h1b = _mix_call(fox_o.reshape(n, FOX_WIDTH), ohg.reshape(n, HG_WIDTH), hgp, h,
                            fox_norm_g[i], hg_norm_g[i], w_out[i].astype(BF16), ln1_g[i], ln1_b[i], tl["rows"])
        keys = peer_keys[i].reshape(PEER_HEADS * 2, PEER_N_KEYS, PEER_HALF_DIM).astype(BF16)
        a_idx, b_idx, gate = _peerq_call(h1b, peer_wq[i].astype(BF16), keys)
        ut = peer_u[i].astype(BF16).T
        act = _peer_act_call(h1b, ut, a_idx, b_idx, tl["act_tb"], tl["act_ec"])
        w = _peer_w_call(act, gate, a_idx, b_idx, tl["w_tb"])
        h, hb = _peer_out_call(w, peer_v[i].astype(BF16), h1, h1b, p[i].reshape(n, PLE_DIM),
                               ple_gate_w[i].astype(BF16), ple_w[i].astype(BF16), ln2_g[i], ln2_b[i],
                               tl["out_tm"], tl["out_tk"])
    return h.reshape(batch, seq, d)
```

```python
import functools

import jax
import jax.numpy as jnp
from jax import lax
from jax.experimental import pallas as pl
from jax.experimental.pallas import tpu as pltpu

D_MODEL = 1024
DEPTH = 2
PLE_DIM = 256
FOX_HEADS = 8
FOX_HEAD_DIM = 64
FOX_WIDTH = FOX_HEADS * FOX_HEAD_DIM
HG_HEADS = 4
HG_KEY_DIM = 128
HG_VAL_DIM = 128
HG_WIDTH = HG_HEADS * HG_VAL_DIM
PEER_HEADS = 8
PEER_N_KEYS = 128
PEER_N_EXPERTS = PEER_N_KEYS * PEER_N_KEYS
PEER_HALF_DIM = 128
PEER_TOPK = 16
PEER_SLOTS = PEER_HEADS * PEER_TOPK
DEEPNORM_ALPHA = (2 * DEPTH) ** 0.25
LN_EPS = 1e-5
RMS_EPS = 1e-6
MASK_VALUE = -1e30
LOG2E = 1.4426950408889634
LB_FLOOR = 1e-30

LANES = 128
SUBLANES = 8
VMEM_LIMIT_BYTES = 56 * 1024 * 1024

FOX_PAIRS_PER_STEP = 1
HG_CHUNK = 16
HG_CUMSUM_ROWS = 128
W_ROW_PITCH = PEER_N_KEYS + SUBLANES
W_TOKEN_GROUP = 16

F32 = jnp.float32
BF16 = jnp.bfloat16


def _params(sem):
    return pltpu.CompilerParams(dimension_semantics=sem, vmem_limit_bytes=VMEM_LIMIT_BYTES)


def _dot(a, b):
    return jnp.dot(a, b, preferred_element_type=F32)


def _dot_nt(a, b):
    return lax.dot_general(a, b, (((1,), (1,)), ((), ())), preferred_element_type=F32)


def _layer_norm_rows(y, g, b):
    mu = jnp.mean(y, axis=-1, keepdims=True)
    d = y - mu
    var = jnp.mean(d * d, axis=-1, keepdims=True)
    return d * lax.rsqrt(var + LN_EPS) * g + b


def _ln_kernel(x_ref, g_ref, b_ref, h_ref, hb_ref):
    h = _layer_norm_rows(x_ref[...], g_ref[...], b_ref[...])
    h_ref[...] = h
    hb_ref[...] = h.astype(BF16)


def _ln_call(x, g, b, tm):
    n, d = x.shape
    return pl.pallas_call(
        _ln_kernel,
        grid=(n // tm,),
        in_specs=[pl.BlockSpec((tm, d), lambda i: (i, 0)),
                  pl.BlockSpec((1, d), lambda i: (0, 0)),
                  pl.BlockSpec((1, d), lambda i: (0, 0))],
        out_specs=[pl.BlockSpec((tm, d), lambda i: (i, 0)),
                   pl.BlockSpec((tm, d), lambda i: (i, 0))],
        out_shape=[jax.ShapeDtypeStruct((n, d), F32), jax.ShapeDtypeStruct((n, d), BF16)],
        compiler_params=_params(("parallel",)),
        name="emb_ln",
    )(x, g.reshape(1, d), b.reshape(1, d))


def _inproj_kernel(h_ref, wqkv_ref, wff_ref, whg_ref, qkv_ref, ff_ref, hg_ref):
    h = h_ref[...]
    qkv_ref[...] = _dot(h, wqkv_ref[...]).astype(BF16)
    ff_ref[...] = _dot(h, wff_ref[...])
    hg_ref[...] = _dot(h, whg_ref[...])


def _inproj_call(hb, wqkv, wff, whg, tm):
    n, d = hb.shape
    return pl.pallas_call(
        _inproj_kernel,
        grid=(n // tm,),
        in_specs=[pl.BlockSpec((tm, d), lambda i: (i, 0)),
                  pl.BlockSpec(wqkv.shape, lambda i: (0, 0)),
                  pl.BlockSpec(wff.shape, lambda i: (0, 0)),
                  pl.BlockSpec(whg.shape, lambda i: (0, 0))],
        out_specs=[pl.BlockSpec((tm, wqkv.shape[1]), lambda i: (i, 0)),
                   pl.BlockSpec((tm, wff.shape[1]), lambda i: (i, 0)),
                   pl.BlockSpec((tm, whg.shape[1]), lambda i: (i, 0))],
        out_shape=[jax.ShapeDtypeStruct((n, wqkv.shape[1]), BF16),
                   jax.ShapeDtypeStruct((n, wff.shape[1]), F32),
                   jax.ShapeDtypeStruct((n, whg.shape[1]), F32)],
        compiler_params=_params(("parallel",)),
        name="in_proj",
    )(hb, wqkv, wff, whg)


def _log_sigmoid(z):
    return jnp.minimum(z, 0.0) - jnp.log(1.0 + jnp.exp(-jnp.abs(z)))


def _fox_cumsum_kernel(ff_ref, fb_ref, c_ref):
    rows, t = ff_ref.shape
    lf = _log_sigmoid(ff_ref[...] + fb_ref[...]) * LOG2E
    r = lax.broadcasted_iota(jnp.int32, (LANES, LANES), 0)
    c = lax.broadcasted_iota(jnp.int32, (LANES, LANES), 1)
    upper = (r <= c).astype(F32).astype(BF16)
    carry = jnp.zeros((rows, 1), F32)
    for j in range(t // LANES):
        blk = lf[:, j * LANES:(j + 1) * LANES]
        cs = sum(_dot(piece.astype(BF16), upper) for piece in _split3(blk))
        cs = cs + carry
        c_ref[:, j * LANES:(j + 1) * LANES] = cs
        carry = cs[:, LANES - 1:LANES]


def _fox_cumsum_call(ff_t, fb_rows):
    rows, t = ff_t.shape
    tr = SUBLANES
    return pl.pallas_call(
        _fox_cumsum_kernel,
        grid=(rows // tr,),
        in_specs=[pl.BlockSpec((tr, t), lambda i: (i, 0)),
                  pl.BlockSpec((tr, 1), lambda i: (i, 0))],
        out_specs=pl.BlockSpec((tr, t), lambda i: (i, 0)),
        out_shape=jax.ShapeDtypeStruct((rows, t), F32),
        compiler_params=_params(("parallel",)),
        name="fox_cumsum",
    )(ff_t, fb_rows)


def _split3(x):
    hi = x.astype(BF16).astype(F32)
    r1 = x - hi
    mid = r1.astype(BF16).astype(F32)
    lo = (r1 - mid).astype(BF16).astype(F32)
    return hi, mid, lo


def _fox_kernel(q_ref, k_ref, v_ref, c_ref, o_ref, kx_ref, vx_ref, qe_ref, *, tq, wide, seq, pairs):
    qi = pl.program_id(2)
    lane = lax.broadcasted_iota(jnp.int32, (1, LANES), 1)
    data = [lane < FOX_HEAD_DIM, lane >= FOX_HEAD_DIM]
    spare = [FOX_HEAD_DIM, 0]
    heads = [(p, h) for p in range(pairs) for h in range(2)]
    slab = lambda p: slice(p * LANES, (p + 1) * LANES)

    sub8 = lax.broadcasted_iota(jnp.int32, (SUBLANES, LANES), 0)

    def bias_tile(h, pieces):
        rows8 = jnp.zeros((SUBLANES, LANES), F32)
        for j, piece in enumerate(pieces):
            rows8 = jnp.where(sub8 == j, piece, rows8)
        blocks = [jnp.zeros((spare[h], LANES), F32)] if spare[h] else []
        blocks += [rows8, jnp.zeros((LANES - spare[h] - SUBLANES, LANES), F32)]
        return jnp.transpose(jnp.concatenate(blocks, axis=0))

    @pl.when(qi == 0)
    def _():
        for n, (p, h) in enumerate(heads):
            for j in range(seq // LANES):
                rows = slice(j * LANES, (j + 1) * LANES)
                hi, mid, lo = _split3(c_ref[p, h:h + 1, rows])
                one = jnp.ones_like(hi)
                k_extra = bias_tile(h, (-hi, -mid, -lo, one, one, one))
                kx_ref[n, rows, :] = jnp.where(data[h], k_ref[rows, slab(p)].astype(F32), k_extra).astype(BF16)
                qe_ref[n, rows, :] = bias_tile(h, (one, one, one, hi, mid, lo)).astype(BF16)
                vx_ref[n, rows, :] = jnp.where(data[h], v_ref[rows, slab(p)].astype(F32), 1.0).astype(BF16)

    q0 = pl.multiple_of(qi * tq, tq)
    qx = []
    for n, (p, h) in enumerate(heads):
        q = q_ref[:, slab(p)].astype(F32)
        q_extra = qe_ref[n, pl.ds(q0, tq), :].astype(F32)
        qx.append((jnp.where(data[h], q, 0.0) + q_extra).astype(BF16))

    def block(k0, width, masked, carry):
        out = []
        for n in range(len(heads)):
            m_old, acc_old = carry[n]
            s = _dot_nt(qx[n], kx_ref[n, pl.ds(k0, width), :])
            if masked:
                row = lax.broadcasted_iota(jnp.int32, (tq, width), 0) + q0
                col = lax.broadcasted_iota(jnp.int32, (tq, width), 1) + k0
                s = jnp.where(col <= row, s, MASK_VALUE)
            m_new = jnp.maximum(m_old, jnp.max(s, axis=-1, keepdims=True))
            alpha = jnp.exp2(m_old - m_new)
            p = jnp.exp2(s - m_new).astype(BF16)
            acc_new = alpha * acc_old + _dot(p, vx_ref[n, pl.ds(k0, width), :])
            out.append((m_new, acc_new))
        return tuple(out)

    init = tuple((jnp.full((tq, 1), MASK_VALUE, F32), jnp.zeros((tq, LANES), F32)) for _ in heads)
    assert wide in (tq, 2 * tq)
    n_wide = q0 // wide
    carry = lax.fori_loop(
        0, n_wide, lambda j, c: block(pl.multiple_of(j * wide, wide), wide, False, c), init)
    if wide == tq:
        carry = block(q0, tq, True, carry)
    else:
        carry = lax.cond(q0 % wide == 0,
                         lambda c: block(q0, tq, True, c),
                         lambda c: block(pl.multiple_of(n_wide * wide, wide), wide, True, c),
                         carry)
    for p in range(pairs):
        res = []
        for h in range(2):
            acc = carry[2 * p + h][1]
            res.append(acc / pltpu.roll(acc, FOX_HEAD_DIM, axis=1))
        o_ref[:, slab(p)] = jnp.where(data[0], res[0], res[1])


def _fox_call(qkv, c, batch, seq, tq, wide, pairs):
    steps = FOX_HEADS // 2 // pairs
    width = pairs * LANES
    return pl.pallas_call(
        functools.partial(_fox_kernel, tq=tq, wide=wide, seq=seq, pairs=pairs),
        grid=(batch, steps, seq // tq),
        in_specs=[pl.BlockSpec((None, tq, width), lambda b, p, i: (b, i, p)),
                  pl.BlockSpec((None, seq, width), lambda b, p, i: (b, 0, steps + p)),
                  pl.BlockSpec((None, seq, width), lambda b, p, i: (b, 0, 2 * steps + p)),
                  pl.BlockSpec((None, pairs, 2, seq), lambda b, p, i: (b, p, 0, 0))],
        out_specs=pl.BlockSpec((None, tq, width), lambda b, p, i: (b, i, p)),
        out_shape=jax.ShapeDtypeStruct((batch, seq, FOX_WIDTH), F32),
        scratch_shapes=[pltpu.VMEM((2 * pairs, seq, LANES), BF16),
                        pltpu.VMEM((2 * pairs, seq, LANES), BF16),
                        pltpu.VMEM((2 * pairs, seq, LANES), BF16)],
        compiler_params=_params(("parallel", "parallel", "arbitrary")),
        name="fox_attention",
    )(qkv, qkv, qkv, c)


def _hgrn_kernel(q_ref, z_ref, v_ref, lbl_ref, o_ref, lf_ref, kk_ref, st_ref, *, seq):
    logits = lbl_ref[...]
    lg = logits[:DEPTH, :]
    mx = jnp.max(lg, axis=0, keepdims=True)
    ex = jnp.exp(lg - mx)
    probs = ex / jnp.sum(ex, axis=0, keepdims=True)
    sel = logits[DEPTH:DEPTH + 1, :]
    lb = jnp.zeros_like(sel)
    run = jnp.zeros_like(sel)
    for i in range(DEPTH):
        run = run + probs[i:i + 1, :]
        lb = jnp.where(sel == float(i), run - probs[0:1, :], lb)
    log_lb = jnp.log(jnp.maximum(lb, LB_FLOOR))
    log_1m = jnp.log(1.0 - lb)

    rr = lax.broadcasted_iota(jnp.int32, (HG_CUMSUM_ROWS, HG_CUMSUM_ROWS), 0)
    cc = lax.broadcasted_iota(jnp.int32, (HG_CUMSUM_ROWS, HG_CUMSUM_ROWS), 1)
    tri = ((rr // HG_CHUNK == cc // HG_CHUNK) & (cc <= rr)).astype(F32).astype(BF16)

    def prep(j, _):
        r0 = pl.multiple_of(j * HG_CUMSUM_ROWS, HG_CUMSUM_ROWS)
        z = z_ref[pl.ds(r0, HG_CUMSUM_ROWS), :]
        a = log_1m + _log_sigmoid(z)
        mm = jnp.maximum(log_lb, a)
        lf = (mm + jnp.log(1.0 + jnp.exp(-jnp.abs(log_lb - a)))) * LOG2E
        lf_ref[pl.ds(r0, HG_CUMSUM_ROWS), :] = sum(_dot(tri, piece.astype(BF16)) for piece in _split3(lf))
        kk_ref[pl.ds(r0, HG_CUMSUM_ROWS), :] = (1.0 - lb) * jax.nn.sigmoid(-z)
        return 0

    lax.fori_loop(0, seq // HG_CUMSUM_ROWS, prep, 0, unroll=2)
    st_ref[...] = jnp.zeros_like(st_ref)
    row8 = lax.broadcasted_iota(jnp.int32, (SUBLANES, 1), 0)

    def chunk(ci, _):
        r0 = pl.multiple_of(ci * HG_CHUNK, HG_CHUNK)
        b = lf_ref[pl.ds(r0, HG_CHUNK), :]
        q = q_ref[pl.ds(r0, HG_CHUNK), :]
        kk = kk_ref[pl.ds(r0, HG_CHUNK), :]
        v = v_ref[pl.ds(r0, HG_CHUNK), :]
        b_last = b[HG_CHUNK - 1:HG_CHUNK, :]
        qe = (q * jnp.exp2(b)).astype(BF16)
        kd = (kk * jnp.exp2(b_last - b)).astype(BF16)
        e_last = jnp.exp2(b_last)
        outs = []
        for h in range(HG_HEADS):
            sl = slice(h * HG_KEY_DIM, (h + 1) * HG_KEY_DIM)
            st = st_ref[h]
            o = _dot_nt(qe[:, sl], st.astype(BF16))
            bh, qh, kh, vh = b[:, sl], q[:, sl], kk[:, sl], v[:, sl]
            halves = []
            for half in range(HG_CHUNK // SUBLANES):
                rows = slice(half * SUBLANES, (half + 1) * SUBLANES)
                bt, qt, acc = bh[rows], qh[rows], o[rows]
                for s in range((half + 1) * SUBLANES):
                    diff = bt - bh[s:s + 1, :]
                    first_live = s - half * SUBLANES
                    if first_live > 0:
                        diff = jnp.where(row8 >= first_live, diff, MASK_VALUE)
                    a_col = jnp.sum(qt * kh[s:s + 1, :] * jnp.exp2(diff), axis=-1, keepdims=True)
                    acc = acc + a_col * vh[s:s + 1, :]
                halves.append(acc)
            o = jnp.concatenate(halves, axis=0)
            upd = lax.dot_general(vh.astype(BF16), kd[:, sl], (((0,), (0,)), ((), ())),
                                  preferred_element_type=F32)
            st_ref[h] = st * e_last[:, sl] + upd
            outs.append(o)
        o_ref[pl.ds(r0, HG_CHUNK), :] = jnp.concatenate(outs, axis=1)
        return 0

    lax.fori_loop(0, seq // HG_CHUNK, chunk, 0, unroll=4)


def _hgrn_call(hgp, lbl, batch, seq):
    return pl.pallas_call(
        functools.partial(_hgrn_kernel, seq=seq),
        grid=(batch,),
        in_specs=[pl.BlockSpec((None, seq, HG_WIDTH), lambda b: (b, 0, 0)),
                  pl.BlockSpec((None, seq, HG_WIDTH), lambda b: (b, 0, 1)),
                  pl.BlockSpec((None, seq, HG_WIDTH), lambda b: (b, 0, 2)),
                  pl.BlockSpec(lbl.shape, lambda b: (0, 0))],
        out_specs=pl.BlockSpec((None, seq, HG_WIDTH), lambda b: (b, 0, 0)),
        out_shape=jax.ShapeDtypeStruct((batch, seq, HG_WIDTH), F32),
        scratch_shapes=[pltpu.VMEM((seq, HG_WIDTH), F32),
                        pltpu.VMEM((seq, HG_WIDTH), F32),
                        pltpu.VMEM((HG_HEADS, HG_VAL_DIM, HG_KEY_DIM), F32)],
        compiler_params=_params(("parallel",)),
        name="hgrn2",
    )(hgp, hgp, hgp, lbl)


def _mix_kernel(fox_ref, ohg_ref, gate_ref, h_ref, gf_ref, gh_ref, wo_ref, g1_ref, b1_ref,
                h1_ref, h1b_ref):
    fox = fox_ref[...]
    fox = fox * lax.rsqrt(jnp.mean(fox * fox, axis=-1, keepdims=True) + RMS_EPS) * gf_ref[...]
    ohg = ohg_ref[...]
    gh = gh_ref[...]
    gate = gate_ref[...]
    parts = [fox.astype(BF16)]
    for h in range(HG_HEADS):
        sl = slice(h * HG_VAL_DIM, (h + 1) * HG_VAL_DIM)
        o = ohg[:, sl]
        o = o * lax.rsqrt(jnp.mean(o * o, axis=-1, keepdims=True) + RMS_EPS) * gh[:, sl]
        g = gate[:, sl]
        parts.append((o * (g * jax.nn.sigmoid(g))).astype(BF16))
    cat = jnp.concatenate(parts, axis=1)
    mix = _dot(cat, wo_ref[...])
    h1 = _layer_norm_rows(DEEPNORM_ALPHA * h_ref[...] + mix, g1_ref[...], b1_ref[...])
    h1_ref[...] = h1
    h1b_ref[...] = h1.astype(BF16)


def _mix_call(fox_o, ohg, hgp, h, gf, gh, wo, g1, b1, tm):
    n, d = h.shape
    row = lambda a: a.reshape(1, -1)
    return pl.pallas_call(
        _mix_kernel,
        grid=(n // tm,),
        in_specs=[pl.BlockSpec((tm, FOX_WIDTH), lambda i: (i, 0)),
                  pl.BlockSpec((tm, HG_WIDTH), lambda i: (i, 0)),
                  pl.BlockSpec((tm, HG_WIDTH), lambda i: (i, 3)),
                  pl.BlockSpec((tm, d), lambda i: (i, 0)),
                  pl.BlockSpec((1, FOX_WIDTH), lambda i: (0, 0)),
                  pl.BlockSpec((1, HG_WIDTH), lambda i: (0, 0)),
                  pl.BlockSpec(wo.shape, lambda i: (0, 0)),
                  pl.BlockSpec((1, d), lambda i: (0, 0)),
                  pl.BlockSpec((1, d), lambda i: (0, 0))],
        out_specs=[pl.BlockSpec((tm, d), lambda i: (i, 0)),
                   pl.BlockSpec((tm, d), lambda i: (i, 0))],
        out_shape=[jax.ShapeDtypeStruct((n, d), F32), jax.ShapeDtypeStruct((n, d), BF16)],
        compiler_params=_params(("parallel",)),
        name="mix_out_ln1",
    )(fox_o, ohg, hgp, h, row(gf), row(gh), wo, row(g1), row(b1))


class _Network:
    def __init__(self):
        self.comparators = []
        self.outputs = []


def _net_merge(net, xs, ys):
    if not xs:
        return list(ys)
    if not ys:
        return list(xs)
    if len(xs) == 1 and len(ys) == 1:
        net.comparators.append((xs[0], ys[0]))
        return [xs[0], ys[0]]
    evens = _net_merge(net, xs[0::2], ys[0::2])
    odds = _net_merge(net, xs[1::2], ys[1::2])
    out = [evens[0]]
    for i, w in enumerate(odds):
        if i + 1 < len(evens):
            net.comparators.append((evens[i + 1], w))
            out += [evens[i + 1], w]
        else:
            out.append(w)
    return out + evens[len(odds) + 1:]


def _net_sort(net, ws):
    if len(ws) <= 1:
        return list(ws)
    mid = len(ws) // 2
    return _net_merge(net, _net_sort(net, ws[:mid]), _net_sort(net, ws[mid:]))


def _top_network(sizes, presorted, keep):
    net = _Network()
    lists, w = [], 0
    for s in sizes:
        ws = list(range(w, w + s))
        w += s
        lists.append(ws if presorted else _net_sort(net, ws))
    while len(lists) > 1:
        lists.sort(key=len)
        lists = [_net_merge(net, lists[0], lists[1])[:keep]] + lists[2:]
    net.outputs = lists[0][:keep]
    needed, kept = set(net.outputs), []
    for a, b in reversed(net.comparators):
        if a in needed or b in needed:
            kept.append((a, b, a in needed, b in needed))
            needed.update((a, b))
    net.comparators = kept[::-1]
    return net


def _run_network(net, wires):
    wires = list(wires)
    for a, b, need_a, need_b in net.comparators:
        (va, ia), (vb, ib) = wires[a], wires[b]
        a_first = (va > vb) | ((va == vb) & (ia < ib))
        if need_a:
            wires[a] = (jnp.maximum(va, vb), jnp.where(a_first, ia, ib))
        if need_b:
            wires[b] = (jnp.minimum(va, vb), jnp.where(a_first, ib, ia))
    return [wires[w] for w in net.outputs]


def _pair_lists():
    ok = lambda j1, j2: (j1 + 1) * (j2 + 1) <= PEER_TOPK
    lists = []
    for d in range(PEER_TOPK):
        row = [(d, j2) for j2 in range(d, PEER_TOPK) if ok(d, j2)]
        col = [(j1, d) for j1 in range(d + 1, PEER_TOPK) if ok(j1, d)]
        lists += [l for l in (row, col) if l]
    return lists


FLAT_SHIFT = 14
_KEY_NET = _top_network([PEER_TOPK] * (PEER_N_KEYS // PEER_TOPK), presorted=False, keep=PEER_TOPK)
_PAIR_LISTS = _pair_lists()
_PAIR_NET = _top_network([len(l) for l in _PAIR_LISTS], presorted=True, keep=PEER_TOPK)


SLAB_TOKENS = SUBLANES * LANES
SLAB_ROWS = SUBLANES * W_ROW_PITCH


def _peerq_kernel(h_ref, wq_ref, keys_ref, a_ref, b_ref, g_ref,
                  sc_ref, tv_ref, ti_ref, oa_ref, ob_ref, og_ref):
    n_lists = 2 * PEER_HEADS
    q = _dot(h_ref[...], wq_ref[...]).astype(BF16)
    for hp in range(n_lists):
        s = _dot_nt(keys_ref[hp], q[:, hp * PEER_HALF_DIM:(hp + 1) * PEER_HALF_DIM])
        for g in range(SUBLANES):
            sc_ref[pl.ds(hp * SLAB_ROWS + g * W_ROW_PITCH, PEER_N_KEYS), :] = s[:, g * LANES:(g + 1) * LANES]

    def keys_top(hp, _):
        base = hp * SLAB_ROWS
        wires = [(sc_ref[pl.ds(base + n, SUBLANES, stride=W_ROW_PITCH), :],
                  jnp.full((SUBLANES, LANES), n, jnp.int32)) for n in range(PEER_N_KEYS)]
        for j, (v, i) in enumerate(_run_network(_KEY_NET, wires)):
            tv_ref[hp * PEER_TOPK + j] = v
            ti_ref[hp * PEER_TOPK + j] = i
        return 0

    lax.fori_loop(0, n_lists, keys_top, 0)

    def pairs_top(h, _):
        r1 = 2 * h * PEER_TOPK
        r2 = r1 + PEER_TOPK
        s1 = [tv_ref[r1 + j] for j in range(PEER_TOPK)]
        s2 = [tv_ref[r2 + j] for j in range(PEER_TOPK)]
        e1 = [ti_ref[r1 + j] * PEER_N_KEYS for j in range(PEER_TOPK)]
        i2 = [ti_ref[r2 + j] for j in range(PEER_TOPK)]
        wires = [(s1[j1] + s2[j2], e1[j1] + i2[j2] + ((j1 * PEER_TOPK + j2) << FLAT_SHIFT))
                 for lst in _PAIR_LISTS for (j1, j2) in lst]
        top = _run_network(_PAIR_NET, wires)
        es = [jnp.exp(v - top[0][0]) for v, _ in top]
        total = es[0]
        for e in es[1:]:
            total = total + e
        inv = 1.0 / total
        for k, ((_, key), e) in enumerate(zip(top, es)):
            expert = key & ((1 << FLAT_SHIFT) - 1)
            rows = pl.ds(h * PEER_TOPK + k, SUBLANES, stride=W_ROW_PITCH)
            oa_ref[rows, :] = lax.shift_right_logical(expert, 7)
            ob_ref[rows, :] = expert & (PEER_N_KEYS - 1)
            og_ref[rows, :] = e * inv
        return 0

    lax.fori_loop(0, PEER_HEADS, pairs_top, 0)
    for g in range(SUBLANES):
        rows = pl.ds(g * W_ROW_PITCH, PEER_SLOTS)
        toks = slice(g * LANES, (g + 1) * LANES)
        a_ref[toks, :] = jnp.transpose(oa_ref[rows, :])
        b_ref[toks, :] = jnp.transpose(ob_ref[rows, :])
        g_ref[toks, :] = jnp.transpose(og_ref[rows, :])


def _peerq_call(h1b, wq, keys):
    n, d = h1b.shape
    tb = SLAB_TOKENS
    slot = lambda dt: jax.ShapeDtypeStruct((n, PEER_SLOTS), dt)
    return pl.pallas_call(
        _peerq_kernel,
        grid=(n // tb,),
        in_specs=[pl.BlockSpec((tb, d), lambda i: (i, 0)),
                  pl.BlockSpec(wq.shape, lambda i: (0, 0)),
                  pl.BlockSpec(keys.shape, lambda i: (0, 0, 0))],
        out_specs=[pl.BlockSpec((tb, PEER_SLOTS), lambda i: (i, 0))] * 3,
        out_shape=[slot(jnp.int32), slot(jnp.int32), slot(F32)],
        scratch_shapes=[pltpu.VMEM((2 * PEER_HEADS * SLAB_ROWS, LANES), F32),
                        pltpu.VMEM((2 * PEER_HEADS * PEER_TOPK, SUBLANES, LANES), F32),
                        pltpu.VMEM((2 * PEER_HEADS * PEER_TOPK, SUBLANES, LANES), jnp.int32),
                        pltpu.VMEM((SLAB_ROWS, LANES), jnp.int32),
                        pltpu.VMEM((SLAB_ROWS, LANES), jnp.int32),
                        pltpu.VMEM((SLAB_ROWS, LANES), F32)],
        compiler_params=_params(("parallel",)),
        name="peer_retrieve",
    )(h1b, wq, keys)


def _peer_act_kernel(x_ref, ut_ref, a_ref, b_ref, o_ref, *, ec):
    j = pl.program_id(1)

    @pl.when(j == 0)
    def _():
        o_ref[...] = jnp.zeros_like(o_ref)

    act = _dot(x_ref[...], ut_ref[...])
    a_idx = a_ref[...]
    b_idx = b_ref[...]
    acc = o_ref[...]
    groups = ec // PEER_N_KEYS
    for g in range(groups):
        i1 = j * groups + g
        picked = jnp.take_along_axis(act[:, g * PEER_N_KEYS:(g + 1) * PEER_N_KEYS], b_idx, axis=1)
        acc = jnp.where(a_idx == i1, picked, acc)
    o_ref[...] = acc


def _peer_act_call(h1b, ut, a_idx, b_idx, tb, ec):
    n, d = h1b.shape
    return pl.pallas_call(
        functools.partial(_peer_act_kernel, ec=ec),
        grid=(n // tb, PEER_N_EXPERTS // ec),
        in_specs=[pl.BlockSpec((tb, d), lambda i, j: (i, 0)),
                  pl.BlockSpec((d, ec), lambda i, j: (0, j)),
                  pl.BlockSpec((tb, PEER_SLOTS), lambda i, j: (i, 0)),
                  pl.BlockSpec((tb, PEER_SLOTS), lambda i, j: (i, 0))],
        out_specs=pl.BlockSpec((tb, PEER_SLOTS), lambda i, j: (i, 0)),
        out_shape=jax.ShapeDtypeStruct((n, PEER_SLOTS), F32),
        compiler_params=_params(("parallel", "arbitrary")),
        name="peer_expert_in",
    )(h1b, ut, a_idx, b_idx)


def _peer_w_kernel(act_ref, g_ref, a_ref, b_ref, w_ref, wt_ref, w3_ref):
    tb = act_ref.shape[0]
    act = act_ref[...]
    gelu = 0.5 * act * (1.0 + lax.erf(act * (2.0 ** -0.5)))
    wt_ref[...] = g_ref[...] * gelu
    sub = lax.broadcasted_iota(jnp.int32, (PEER_N_KEYS, PEER_SLOTS), 0)

    def build_pair(w3_ref, g0, j):
        pts, qts = [], []
        for t in (g0 + j, g0 + j + 1):
            pts.append(jnp.where(sub == a_ref[t:t + 1, :], wt_ref[t:t + 1, :], 0.0).astype(BF16))
            qts.append(jnp.where(sub == b_ref[t:t + 1, :], 1.0, 0.0).astype(BF16))
        zero = jnp.zeros_like(qts[0])
        rhs_t = jnp.concatenate([jnp.concatenate([qts[0], zero], axis=1),
                                 jnp.concatenate([zero, qts[1]], axis=1)], axis=0)
        both = _dot_nt(jnp.concatenate(pts, axis=1), rhs_t)
        for k in range(2):
            r0 = (j + k) * W_ROW_PITCH
            w3_ref[r0:r0 + PEER_N_KEYS, :] = both[:, k * PEER_N_KEYS:(k + 1) * PEER_N_KEYS]

    def relayout(w3_ref, g0, i1):
        rows = w3_ref[pl.ds(i1, W_TOKEN_GROUP, stride=W_ROW_PITCH), :]
        w_ref[g0:g0 + W_TOKEN_GROUP, i1 * PEER_N_KEYS:(i1 + 1) * PEER_N_KEYS] = rows.astype(BF16)

    for g0 in range(0, tb, W_TOKEN_GROUP):
        for j in range(0, W_TOKEN_GROUP, 2):
            build_pair(w3_ref, g0, j)
        for i1 in range(PEER_N_KEYS):
            relayout(w3_ref, g0, i1)


def _peer_w_call(act, gate, a_idx, b_idx, tb):
    n = act.shape[0]
    spec = pl.BlockSpec((tb, PEER_SLOTS), lambda i: (i, 0))
    return pl.pallas_call(
        _peer_w_kernel,
        grid=(n // tb,),
        in_specs=[spec, spec, spec, spec],
        out_specs=pl.BlockSpec((tb, PEER_N_EXPERTS), lambda i: (i, 0)),
        out_shape=jax.ShapeDtypeStruct((n, PEER_N_EXPERTS), BF16),
        scratch_shapes=[pltpu.VMEM((tb, PEER_SLOTS), F32),
                        pltpu.VMEM((W_TOKEN_GROUP * W_ROW_PITCH, PEER_N_KEYS), F32)],
        compiler_params=_params(("parallel",)),
        name="peer_expert_weights",
    )(act, gate, a_idx, b_idx)


def _peer_out_kernel(w_ref, v_ref, h1_ref, h1b_ref, p_ref, gw_ref, pw_ref, g2_ref, b2_ref,
                     h2_ref, h2b_ref, acc_ref):
    j = pl.program_id(1)

    @pl.when(j == 0)
    def _():
        acc_ref[...] = jnp.zeros_like(acc_ref)

    acc_ref[...] += _dot(w_ref[...], v_ref[...])

    @pl.when(j == pl.num_programs(1) - 1)
    def _():
        gate = jax.nn.sigmoid(_dot(h1b_ref[...], gw_ref[...]))
        ple = gate * _dot(p_ref[...].astype(BF16), pw_ref[...])
        y = DEEPNORM_ALPHA * h1_ref[...] + acc_ref[...] + ple
        h2 = _layer_norm_rows(y, g2_ref[...], b2_ref[...])
        h2_ref[...] = h2
        h2b_ref[...] = h2.astype(BF16)


def _peer_out_call(w, v, h1, h1b, p, gw, pw, g2, b2, tm, tk):
    n, d = h1.shape
    row = lambda a: a.reshape(1, -1)
    return pl.pallas_call(
        _peer_out_kernel,
        grid=(n // tm, PEER_N_EXPERTS // tk),
        in_specs=[pl.BlockSpec((tm, tk), lambda i, j: (i, j)),
                  pl.BlockSpec((tk, d), lambda i, j: (j, 0)),
                  pl.BlockSpec((tm, d), lambda i, j: (i, 0)),
                  pl.BlockSpec((tm, d), lambda i, j: (i, 0)),
                  pl.BlockSpec((tm, PLE_DIM), lambda i, j: (i, 0)),
                  pl.BlockSpec(gw.shape, lambda i, j: (0, 0)),
                  pl.BlockSpec(pw.shape, lambda i, j: (0, 0)),
                  pl.BlockSpec((1, d), lambda i, j: (0, 0)),
                  pl.BlockSpec((1, d), lambda i, j: (0, 0))],
        out_specs=[pl.BlockSpec((tm, d), lambda i, j: (i, 0)),
                   pl.BlockSpec((tm, d), lambda i, j: (i, 0))],
        out_shape=[jax.ShapeDtypeStruct((n, d), F32), jax.ShapeDtypeStruct((n, d), BF16)],
        scratch_shapes=[pltpu.VMEM((tm, d), F32)],
        compiler_params=_params(("parallel", "arbitrary")),
        name="peer_out_ple_ln2",
    )(w, v, h1, h1b, p, gw, pw, row(g2), row(b2))


def _tiles(n, seq):
    pick = lambda pref, total: pref if total % pref == 0 else total
    return dict(
        rows=pick(512, n),
        tq=pick(512, seq), wide=pick(1024, seq),
        act_tb=pick(1024, n), act_ec=4096,
        w_tb=pick(128, n),
        out_tm=pick(512, n), out_tk=4096,
    )


def kernel(x, p, emb_ln_g, emb_ln_b, w_in, fox_fb, fox_norm_g, hg_lb_logits, hg_norm_g, w_out,
           ln1_g, ln1_b, peer_wq, peer_keys, peer_u, peer_v, ple_gate_w, ple_w, ln2_g, ln2_b):
    batch, seq, d = x.shape
    n = batch * seq
    assert n % SLAB_TOKENS == 0, "token count must be a multiple of the retrieval block"
    tl = _tiles(n, seq)
    h, hb = _ln_call(x.reshape(n, d), emb_ln_g, emb_ln_b, tl["rows"])
    scale = FOX_HEAD_DIM ** -0.5 * LOG2E
    o_ff = 3 * FOX_WIDTH
    o_hg = o_ff + FOX_HEADS
    for i in range(DEPTH):
        wi = w_in[i]
        wqkv = jnp.concatenate([wi[:, :FOX_WIDTH] * scale, wi[:, FOX_WIDTH:o_ff]], axis=1).astype(BF16)
        wff = jnp.pad(wi[:, o_ff:o_hg], ((0, 0), (0, LANES - FOX_HEADS))).astype(BF16)
        whg = wi[:, o_hg:].astype(BF16)
        qkv, ff, hgp = _inproj_call(hb, wqkv, wff, whg, tl["rows"])
        ff_t = ff[:, :FOX_HEADS].reshape(batch, seq, FOX_HEADS).transpose(0, 2, 1).reshape(batch * FOX_HEADS, seq)
        fb_rows = jnp.tile(fox_fb[i], batch).reshape(batch * FOX_HEADS, 1)
        c = _fox_cumsum_call(ff_t, fb_rows).reshape(batch, FOX_HEADS // 2, 2, seq)
        fox_o = _fox_call(qkv.reshape(batch, seq, 3 * FOX_WIDTH), c, batch, seq, tl["tq"], tl["wide"], FOX_PAIRS_PER_STEP)
        lbl = jnp.concatenate([hg_lb_logits.astype(F32), jnp.full((1, HG_WIDTH), float(i), F32)], axis=0)
        ohg = _hgrn_call(hgp.reshape(batch, seq, 4 * HG_WIDTH), lbl, batch, seq)
        h1, h1b = _mix_call(fox_o.reshape(n, FOX_WIDTH), ohg.reshape(n, HG_WIDTH), hgp, h,
                            fox_norm_g[i], hg_norm_g[i], w_out[i].astype(BF16), ln1_g[i], ln1_b[i], tl["rows"])
        keys = peer_keys[i].reshape(PEER_HEADS * 2, PEER_N_KEYS, PEER_HALF_DIM).astype(BF16)
        a_idx, b_idx, gate = _peerq_call(h1b, peer_wq[i].astype(BF16), keys)
        ut = peer_u[i].astype(BF16).T
        act = _peer_act_call(h1b, ut, a_idx, b_idx, tl["act_tb"], tl["act_ec"])
        w = _peer_w_call(act, gate, a_idx, b_idx, tl["w_tb"])
        h, hb = _peer_out_call(w, peer_v[i].astype(BF16), h1, h1b, p[i].reshape(n, PLE_DIM),
                               ple_gate_w[i].astype(BF16), ple_w[i].astype(BF16), ln2_g[i], ln2_b[i],
                               tl["out_tm"], tl["out_tk"])
    return h.reshape(batch, seq, d)
```

```python
import functools

import jax
import jax.numpy as jnp
from jax import lax
from jax.experimental import pallas as pl
from jax.experimental.pallas import tpu as pltpu

D_MODEL = 1024
DEPTH = 2
PLE_DIM = 256
FOX_HEADS = 8
FOX_HEAD_DIM = 64
FOX_WIDTH = FOX_HEADS * FOX_HEAD_DIM
HG_HEADS = 4
HG_KEY_DIM = 128
HG_VAL_DIM = 128
HG_WIDTH = HG_HEADS * HG_VAL_DIM
PEER_HEADS = 8
PEER_N_KEYS = 128
PEER_N_EXPERTS = PEER_N_KEYS * PEER_N_KEYS
PEER_HALF_DIM = 128
PEER_TOPK = 16
PEER_SLOTS = PEER_HEADS * PEER_TOPK
DEEPNORM_ALPHA = (2 * DEPTH) ** 0.25
LN_EPS = 1e-5
RMS_EPS = 1e-6
MASK_VALUE = -1e30
LOG2E = 1.4426950408889634
LB_FLOOR = 1e-30

LANES = 128
SUBLANES = 8
VMEM_LIMIT_BYTES = 56 * 1024 * 1024

FOX_PAIRS_PER_STEP = 1
HG_CHUNK = 16
HG_CUMSUM_ROWS = 128
W_ROW_PITCH = PEER_N_KEYS + SUBLANES
W_TOKEN_GROUP = 16

F32 = jnp.float32
BF16 = jnp.bfloat16


def _params(sem):
    return pltpu.CompilerParams(dimension_semantics=sem, vmem_limit_bytes=VMEM_LIMIT_BYTES)


def _dot(a, b):
    return jnp.dot(a, b, preferred_element_type=F32)


def _dot_nt(a, b):
    return lax.dot_general(a, b, (((1,), (1,)), ((), ())), preferred_element_type=F32)


def _layer_norm_rows(y, g, b):
    mu = jnp.mean(y, axis=-1, keepdims=True)
    d = y - mu
    var = jnp.mean(d * d, axis=-1, keepdims=True)
    return d * lax.rsqrt(var + LN_EPS) * g + b


def _ln_kernel(x_ref, g_ref, b_ref, h_ref, hb_ref):
    h = _layer_norm_rows(x_ref[...], g_ref[...], b_ref[...])
    h_ref[...] = h
    hb_ref[...] = h.astype(BF16)


def _ln_call(x, g, b, tm):
    n, d = x.shape
    return pl.pallas_call(
        _ln_kernel,
        grid=(n // tm,),
        in_specs=[pl.BlockSpec((tm, d), lambda i: (i, 0)),
                  pl.BlockSpec((1, d), lambda i: (0, 0)),
                  pl.BlockSpec((1, d), lambda i: (0, 0))],
        out_specs=[pl.BlockSpec((tm, d), lambda i: (i, 0)),
                   pl.BlockSpec((tm, d), lambda i: (i, 0))],
        out_shape=[jax.ShapeDtypeStruct((n, d), F32), jax.ShapeDtypeStruct((n, d), BF16)],
        compiler_params=_params(("parallel",)),
        name="emb_ln",
    )(x, g.reshape(1, d), b.reshape(1, d))


def _inproj_kernel(h_ref, wqkv_ref, wff_ref, whg_ref, qkv_ref, ff_ref, hg_ref):
    h = h_ref[...]
    qkv_ref[...] = _dot(h, wqkv_ref[...]).astype(BF16)
    ff_ref[...] = _dot(h, wff_ref[...])
    hg_ref[...] = _dot(h, whg_ref[...])


def _inproj_call(hb, wqkv, wff, whg, tm):
    n, d = hb.shape
    return pl.pallas_call(
        _inproj_kernel,
        grid=(n // tm,),
        in_specs=[pl.BlockSpec((tm, d), lambda i: (i, 0)),
                  pl.BlockSpec(wqkv.shape, lambda i: (0, 0)),
                  pl.BlockSpec(wff.shape, lambda i: (0, 0)),
                  pl.BlockSpec(whg.shape, lambda i: (0, 0))],
        out_specs=[pl.BlockSpec((tm, wqkv.shape[1]), lambda i: (i, 0)),
                   pl.BlockSpec((tm, wff.shape[1]), lambda i: (i, 0)),
                   pl.BlockSpec((tm, whg.shape[1]), lambda i: (i, 0))],
        out_shape=[jax.ShapeDtypeStruct((n, wqkv.shape[1]), BF16),
                   jax.ShapeDtypeStruct((n, wff.shape[1]), F32),
                   jax.ShapeDtypeStruct((n, whg.shape[1]), F32)],
        compiler_params=_params(("parallel",)),
        name="in_proj",
    )(hb, wqkv, wff, whg)


def _log_sigmoid(z):
    return jnp.minimum(z, 0.0) - jnp.log(1.0 + jnp.exp(-jnp.abs(z)))


def _fox_cumsum_kernel(ff_ref, fb_ref, c_ref):
    rows, t = ff_ref.shape
    lf = _log_sigmoid(ff_ref[...] + fb_ref[...]) * LOG2E
    r = lax.broadcasted_iota(jnp.int32, (LANES, LANES), 0)
    c = lax.broadcasted_iota(jnp.int32, (LANES, LANES), 1)
    upper = (r <= c).astype(F32).astype(BF16)
    carry = jnp.zeros((rows, 1), F32)
    for j in range(t // LANES):
        blk = lf[:, j * LANES:(j + 1) * LANES]
        cs = sum(_dot(piece.astype(BF16), upper) for piece in _split3(blk))
        cs = cs + carry
        c_ref[:, j * LANES:(j + 1) * LANES] = cs
        carry = cs[:, LANES - 1:LANES]


def _fox_cumsum_call(ff_t, fb_rows):
    rows, t = ff_t.shape
    tr = SUBLANES
    return pl.pallas_call(
        _fox_cumsum_kernel,
        grid=(rows // tr,),
        in_specs=[pl.BlockSpec((tr, t), lambda i: (i, 0)),
                  pl.BlockSpec((tr, 1), lambda i: (i, 0))],
        out_specs=pl.BlockSpec((tr, t), lambda i: (i, 0)),
        out_shape=jax.ShapeDtypeStruct((rows, t), F32),
        compiler_params=_params(("parallel",)),
        name="fox_cumsum",
    )(ff_t, fb_rows)


def _split3(x):
    hi = x.astype(BF16).astype(F32)
    r1 = x - hi
    mid = r1.astype(BF16).astype(F32)
    lo = (r1 - mid).astype(BF16).astype(F32)
    return hi, mid, lo


def _fox_kernel(q_ref, k_ref, v_ref, c_ref, o_ref, kx_ref, vx_ref, qe_ref, *, tq, wide, seq, pairs):
    qi = pl.program_id(2)
    lane = lax.broadcasted_iota(jnp.int32, (1, LANES), 1)
    data = [lane < FOX_HEAD_DIM, lane >= FOX_HEAD_DIM]
    spare = [FOX_HEAD_DIM, 0]
    heads = [(p, h) for p in range(pairs) for h in range(2)]
    slab = lambda p: slice(p * LANES, (p + 1) * LANES)

    sub8 = lax.broadcasted_iota(jnp.int32, (SUBLANES, LANES), 0)

    def bias_tile(h, pieces):
        rows8 = jnp.zeros((SUBLANES, LANES), F32)
        for j, piece in enumerate(pieces):
            rows8 = jnp.where(sub8 == j, piece, rows8)
        blocks = [jnp.zeros((spare[h], LANES), F32)] if spare[h] else []
        blocks += [rows8, jnp.zeros((LANES - spare[h] - SUBLANES, LANES), F32)]
        return jnp.transpose(jnp.concatenate(blocks, axis=0))

    @pl.when(qi == 0)
    def _():
        for n, (p, h) in enumerate(heads):
            for j in range(seq // LANES):
                rows = slice(j * LANES, (j + 1) * LANES)
                hi, mid, lo = _split3(c_ref[p, h:h + 1, rows])
                one = jnp.ones_like(hi)
                k_extra = bias_tile(h, (-hi, -mid, -lo, one, one, one))
                kx_ref[n, rows, :] = jnp.where(data[h], k_ref[rows, slab(p)].astype(F32), k_extra).astype(BF16)
                qe_ref[n, rows, :] = bias_tile(h, (one, one, one, hi, mid, lo)).astype(BF16)
                vx_ref[n, rows, :] = jnp.where(data[h], v_ref[rows, slab(p)].astype(F32), 1.0).astype(BF16)

    q0 = pl.multiple_of(qi * tq, tq)
    qx = []
    for n, (p, h) in enumerate(heads):
        q = q_ref[:, slab(p)].astype(F32)
        q_extra = qe_ref[n, pl.ds(q0, tq), :].astype(F32)
        qx.append((jnp.where(data[h], q, 0.0) + q_extra).astype(BF16))

    def block(k0, width, masked, carry):
        out = []
        for n in range(len(heads)):
            m_old, acc_old = carry[n]
            s = _dot_nt(qx[n], kx_ref[n, pl.ds(k0, width), :])
            if masked:
                row = lax.broadcasted_iota(jnp.int32, (tq, width), 0) + q0
                col = lax.broadcasted_iota(jnp.int32, (tq, width), 1) + k0
                s = jnp.where(col <= row, s, MASK_VALUE)
            m_new = jnp.maximum(m_old, jnp.max(s, axis=-1, keepdims=True))
            alpha = jnp.exp2(m_old - m_new)
            p = jnp.exp2(s - m_new).astype(BF16)
            acc_new = alpha * acc_old + _dot(p, vx_ref[n, pl.ds(k0, width), :])
            out.append((m_new, acc_new))
        return tuple(out)

    init = tuple((jnp.full((tq, 1), MASK_VALUE, F32), jnp.zeros((tq, LANES), F32)) for _ in heads)
    assert wide in (tq, 2 * tq)
    n_wide = q0 // wide
    carry = lax.fori_loop(
        0, n_wide, lambda j, c: block(pl.multiple_of(j * wide, wide), wide, False, c), init)
    if wide == tq:
        carry = block(q0, tq, True, carry)
    else:
        carry = lax.cond(q0 % wide == 0,
                         lambda c: block(q0, tq, True, c),
                         lambda c: block(pl.multiple_of(n_wide * wide, wide), wide, True, c),
                         carry)
    for p in range(pairs):
        res = []
        for h in range(2):
            acc = carry[2 * p + h][1]
            res.append(acc / pltpu.roll(acc, FOX_HEAD_DIM, axis=1))
        o_ref[:, slab(p)] = jnp.where(data[0], res[0], res[1])


def _fox_call(qkv, c, batch, seq, tq, wide, pairs):
    steps = FOX_HEADS // 2 // pairs
    width = pairs * LANES
    return pl.pallas_call(
        functools.partial(_fox_kernel, tq=tq, wide=wide, seq=seq, pairs=pairs),
        grid=(batch, steps, seq // tq),
        in_specs=[pl.BlockSpec((None, tq, width), lambda b, p, i: (b, i, p)),
                  pl.BlockSpec((None, seq, width), lambda b, p, i: (b, 0, steps + p)),
                  pl.BlockSpec((None, seq, width), lambda b, p, i: (b, 0, 2 * steps + p)),
                  pl.BlockSpec((None, pairs, 2, seq), lambda b, p, i: (b, p, 0, 0))],
        out_specs=pl.BlockSpec((None, tq, width), lambda b, p, i: (b, i, p)),
        out_shape=jax.ShapeDtypeStruct((batch, seq, FOX_WIDTH), F32),
        scratch_shapes=[pltpu.VMEM((2 * pairs, seq, LANES), BF16),
                        pltpu.VMEM((2 * pairs, seq, LANES), BF16),
                        pltpu.VMEM((2 * pairs, seq, LANES), BF16)],
        compiler_params=_params(("parallel", "parallel", "arbitrary")),
        name="fox_attention",
    )(qkv, qkv, qkv, c)


def _hgrn_kernel(q_ref, z_ref, v_ref, lbl_ref, o_ref, lf_ref, kk_ref, st_ref, *, seq):
    logits = lbl_ref[...]
    lg = logits[:DEPTH, :]
    mx = jnp.max(lg, axis=0, keepdims=True)
    ex = jnp.exp(lg - mx)
    probs = ex / jnp.sum(ex, axis=0, keepdims=True)
    sel = logits[DEPTH:DEPTH + 1, :]
    lb = jnp.zeros_like(sel)
    run = jnp.zeros_like(sel)
    for i in range(DEPTH):
        run = run + probs[i:i + 1, :]
        lb = jnp.where(sel == float(i), run - probs[0:1, :], lb)
    log_lb = jnp.log(jnp.maximum(lb, LB_FLOOR))
    log_1m = jnp.log(1.0 - lb)

    rr = lax.broadcasted_iota(jnp.int32, (HG_CUMSUM_ROWS, HG_CUMSUM_ROWS), 0)
    cc = lax.broadcasted_iota(jnp.int32, (HG_CUMSUM_ROWS, HG_CUMSUM_ROWS), 1)
    tri = ((rr // HG_CHUNK == cc // HG_CHUNK) & (cc <= rr)).astype(F32).astype(BF16)

    def prep(j, _):
        r0 = pl.multiple_of(j * HG_CUMSUM_ROWS, HG_CUMSUM_ROWS)
        z = z_ref[pl.ds(r0, HG_CUMSUM_ROWS), :]
        a = log_1m + _log_sigmoid(z)
        mm = jnp.maximum(log_lb, a)
        lf = (mm + jnp.log(1.0 + jnp.exp(-jnp.abs(log_lb - a)))) * LOG2E
        lf_ref[pl.ds(r0, HG_CUMSUM_ROWS), :] = sum(_dot(tri, piece.astype(BF16)) for piece in _split3(lf))
        kk_ref[pl.ds(r0, HG_CUMSUM_ROWS), :] = (1.0 - lb) * jax.nn.sigmoid(-z)
        return 0

    lax.fori_loop(0, seq // HG_CUMSUM_ROWS, prep, 0, unroll=2)
    st_ref[...] = jnp.zeros_like(st_ref)
    row8 = lax.broadcasted_iota(jnp.int32, (SUBLANES, 1), 0)

    def chunk(ci, _):
        r0 = pl.multiple_of(ci * HG_CHUNK, HG_CHUNK)
        b = lf_ref[pl.ds(r0, HG_CHUNK), :]
        q = q_ref[pl.ds(r0, HG_CHUNK), :]
        kk = kk_ref[pl.ds(r0, HG_CHUNK), :]
        v = v_ref[pl.ds(r0, HG_CHUNK), :]
        b_last = b[HG_CHUNK - 1:HG_CHUNK, :]
        qe = (q * jnp.exp2(b)).astype(BF16)
        kd = (kk * jnp.exp2(b_last - b)).astype(BF16)
        e_last = jnp.exp2(b_last)
        outs = []
        for h in range(HG_HEADS):
            sl = slice(h * HG_KEY_DIM, (h + 1) * HG_KEY_DIM)
            st = st_ref[h]
            o = _dot_nt(qe[:, sl], st.astype(BF16))
            bh, qh, kh, vh = b[:, sl], q[:, sl], kk[:, sl], v[:, sl]
            halves = []
            for half in range(HG_CHUNK // SUBLANES):
                rows = slice(half * SUBLANES, (half + 1) * SUBLANES)
                bt, qt, acc = bh[rows], qh[rows], o[rows]
                for s in range((half + 1) * SUBLANES):
                    diff = bt - bh[s:s + 1, :]
                    first_live = s - half * SUBLANES
                    if first_live > 0:
                        diff = jnp.where(row8 >= first_live, diff, MASK_VALUE)
                    a_col = jnp.sum(qt * kh[s:s + 1, :] * jnp.exp2(diff), axis=-1, keepdims=True)
                    acc = acc + a_col * vh[s:s + 1, :]
                halves.append(acc)
            o = jnp.concatenate(halves, axis=0)
            upd = lax.dot_general(vh.astype(BF16), kd[:, sl], (((0,), (0,)), ((), ())),
                                  preferred_element_type=F32)
            st_ref[h] = st * e_last[:, sl] + upd
            outs.append(o)
        o_ref[pl.ds(r0, HG_CHUNK), :] = jnp.concatenate(outs, axis=1)
        return 0

    lax.fori_loop(0, seq // HG_CHUNK, chunk, 0, unroll=4)


def _hgrn_call(hgp, lbl, batch, seq):
    return pl.pallas_call(
        functools.partial(_hgrn_kernel, seq=seq),
        grid=(batch,),
        in_specs=[pl.BlockSpec((None, seq, HG_WIDTH), lambda b: (b, 0, 0)),
                  pl.BlockSpec((None, seq, HG_WIDTH), lambda b: (b, 0, 1)),
                  pl.BlockSpec((None, seq, HG_WIDTH), lambda b: (b, 0, 2)),
                  pl.BlockSpec(lbl.shape, lambda b: (0, 0))],
        out_specs=pl.BlockSpec((None, seq, HG_WIDTH), lambda b: (b, 0, 0)),
        out_shape=jax.ShapeDtypeStruct((batch, seq, HG_WIDTH), F32),
        scratch_shapes=[pltpu.VMEM((seq, HG_WIDTH), F32),
                        pltpu.VMEM((seq, HG_WIDTH), F32),
                        pltpu.VMEM((HG_HEADS, HG_VAL_DIM, HG_KEY_DIM), F32)],
        compiler_params=_params(("parallel",)),
        name="hgrn2",
    )(hgp, hgp, hgp, lbl)


def _mix_kernel(fox_ref, ohg_ref, gate_ref, h_ref, gf_ref, gh_ref, wo_ref, g1_ref, b1_ref,
                h1_ref, h1b_ref):
    fox = fox_ref[...]
    fox = fox * lax.rsqrt(jnp.mean(fox * fox, axis=-1, keepdims=True) + RMS_EPS) * gf_ref[...]
    ohg = ohg_ref[...]
    gh = gh_ref[...]
    gate = gate_ref[...]
    parts = [fox.astype(BF16)]
    for h in range(HG_HEADS):
        sl = slice(h * HG_VAL_DIM, (h + 1) * HG_VAL_DIM)
        o = ohg[:, sl]
        o = o * lax.rsqrt(jnp.mean(o * o, axis=-1, keepdims=True) + RMS_EPS) * gh[:, sl]
        g = gate[:, sl]
        parts.append((o * (g * jax.nn.sigmoid(g))).astype(BF16))
    cat = jnp.concatenate(parts, axis=1)
    mix = _dot(cat, wo_ref[...])
    h1 = _layer_norm_rows(DEEPNORM_ALPHA * h_ref[...] + mix, g1_ref[...], b1_ref[...])
    h1_ref[...] = h1
    h1b_ref[...] = h1.astype(BF16)


def _mix_call(fox_o, ohg, hgp, h, gf, gh, wo, g1, b1, tm):
    n, d = h.shape
    row = lambda a: a.reshape(1, -1)
    return pl.pallas_call(
        _mix_kernel,
        grid=(n // tm,),
        in_specs=[pl.BlockSpec((tm, FOX_WIDTH), lambda i: (i, 0)),
                  pl.BlockSpec((tm, HG_WIDTH), lambda i: (i, 0)),
                  pl.BlockSpec((tm, HG_WIDTH), lambda i: (i, 3)),
                  pl.BlockSpec((tm, d), lambda i: (i, 0)),
                  pl.BlockSpec((1, FOX_WIDTH), lambda i: (0, 0)),
                  pl.BlockSpec((1, HG_WIDTH), lambda i: (0, 0)),
                  pl.BlockSpec(wo.shape, lambda i: (0, 0)),
                  pl.BlockSpec((1, d), lambda i: (0, 0)),
                  pl.BlockSpec((1, d), lambda i: (0, 0))],
        out_specs=[pl.BlockSpec((tm, d), lambda i: (i, 0)),
                   pl.BlockSpec((tm, d), lambda i: (i, 0))],
        out_shape=[jax.ShapeDtypeStruct((n, d), F32), jax.ShapeDtypeStruct((n, d), BF16)],
        compiler_params=_params(("parallel",)),
        name="mix_out_ln1",
    )(fox_o, ohg, hgp, h, row(gf), row(gh), wo, row(g1), row(b1))


class _Network:
    def __init__(self):
        self.comparators = []
        self.outputs = []


def _net_merge(net, xs, ys):
    if not xs:
        return list(ys)
    if not ys:
        return list(xs)
    if len(xs) == 1 and len(ys) == 1:
        net.comparators.append((xs[0], ys[0]))
        return [xs[0], ys[0]]
    evens = _net_merge(net, xs[0::2], ys[0::2])
    odds = _net_merge(net, xs[1::2], ys[1::2])
    out = [evens[0]]
    for i, w in enumerate(odds):
        if i + 1 < len(evens):
            net.comparators.append((evens[i + 1], w))
            out += [evens[i + 1], w]
        else:
            out.append(w)
    return out + evens[len(odds) + 1:]


def _net_sort(net, ws):
    if len(ws) <= 1:
        return list(ws)
    mid = len(ws) // 2
    return _net_merge(net, _net_sort(net, ws[:mid]), _net_sort(net, ws[mid:]))


def _top_network(sizes, presorted, keep):
    net = _Network()
    lists, w = [], 0
    for s in sizes:
        ws = list(range(w, w + s))
        w += s
        lists.append(ws if presorted else _net_sort(net, ws))
    while len(lists) > 1:
        lists.sort(key=len)
        lists = [_net_merge(net, lists[0], lists[1])[:keep]] + lists[2:]
    net.outputs = lists[0][:keep]
    needed, kept = set(net.outputs), []
    for a, b in reversed(net.comparators):
        if a in needed or b in needed:
            kept.append((a, b, a in needed, b in needed))
            needed.update((a, b))
    net.comparators = kept[::-1]
    return net


def _run_network(net, wires):
    wires = list(wires)
    for a, b, need_a, need_b in net.comparators:
        (va, ia), (vb, ib) = wires[a], wires[b]
        a_first = (va > vb) | ((va == vb) & (ia < ib))
        if need_a:
            wires[a] = (jnp.maximum(va, vb), jnp.where(a_first, ia, ib))
        if need_b:
            wires[b] = (jnp.minimum(va, vb), jnp.where(a_first, ib, ia))
    return [wires[w] for w in net.outputs]


def _pair_lists():
    ok = lambda j1, j2: (j1 + 1) * (j2 + 1) <= PEER_TOPK
    lists = []
    for d in range(PEER_TOPK):
        row = [(d, j2) for j2 in range(d, PEER_TOPK) if ok(d, j2)]
        col = [(j1, d) for j1 in range(d + 1, PEER_TOPK) if ok(j1, d)]
        lists += [l for l in (row, col) if l]
    return lists


FLAT_SHIFT = 14
_KEY_NET = _top_network([PEER_TOPK] * (PEER_N_KEYS // PEER_TOPK), presorted=False, keep=PEER_TOPK)
_PAIR_LISTS = _pair_lists()
_PAIR_NET = _top_network([len(l) for l in _PAIR_LISTS], presorted=True, keep=PEER_TOPK)


SLAB_TOKENS = SUBLANES * LANES
SLAB_ROWS = SUBLANES * W_ROW_PITCH


def _peerq_kernel(h_ref, wq_ref, keys_ref, a_ref, b_ref, g_ref,
                  sc_ref, tv_ref, ti_ref, oa_ref, ob_ref, og_ref):
    n_lists = 2 * PEER_HEADS
    q = _dot(h_ref[...], wq_ref[...]).astype(BF16)
    for hp in range(n_lists):
        s = _dot_nt(keys_ref[hp], q[:, hp * PEER_HALF_DIM:(hp + 1) * PEER_HALF_DIM])
        for g in range(SUBLANES):
            sc_ref[pl.ds(hp * SLAB_ROWS + g * W_ROW_PITCH, PEER_N_KEYS), :] = s[:, g * LANES:(g + 1) * LANES]

    def keys_top(hp, _):
        base = hp * SLAB_ROWS
        wires = [(sc_ref[pl.ds(base + n, SUBLANES, stride=W_ROW_PITCH), :],
                  jnp.full((SUBLANES, LANES), n, jnp.int32)) for n in range(PEER_N_KEYS)]
        for j, (v, i) in enumerate(_run_network(_KEY_NET, wires)):
            tv_ref[hp * PEER_TOPK + j] = v
            ti_ref[hp * PEER_TOPK + j] = i
        return 0

    lax.fori_loop(0, n_lists, keys_top, 0)

    def pairs_top(h, _):
        r1 = 2 * h * PEER_TOPK
        r2 = r1 + PEER_TOPK
        s1 = [tv_ref[r1 + j] for j in range(PEER_TOPK)]
        s2 = [tv_ref[r2 + j] for j in range(PEER_TOPK)]
        e1 = [ti_ref[r1 + j] * PEER_N_KEYS for j in range(PEER_TOPK)]
        i2 = [ti_ref[r2 + j] for j in range(PEER_TOPK)]
        wires = [(s1[j1] + s2[j2], e1[j1] + i2[j2] + ((j1 * PEER_TOPK + j2) << FLAT_SHIFT))
                 for lst in _PAIR_LISTS for (j1, j2) in lst]
        top = _run_network(_PAIR_NET, wires)
        es = [jnp.exp(v - top[0][0]) for v, _ in top]
        total = es[0]
        for e in es[1:]:
            total = total + e
        inv = 1.0 / total
        for k, ((_, key), e) in enumerate(zip(top, es)):
            expert = key & ((1 << FLAT_SHIFT) - 1)
            rows = pl.ds(h * PEER_TOPK + k, SUBLANES, stride=W_ROW_PITCH)
            oa_ref[rows, :] = lax.shift_right_logical(expert, 7)
            ob_ref[rows, :] = expert & (PEER_N_KEYS - 1)
            og_ref[rows, :] = e * inv
        return 0

    lax.fori_loop(0, PEER_HEADS, pairs_top, 0)
    for g in range(SUBLANES):
        rows = pl.ds(g * W_ROW_PITCH, PEER_SLOTS)
        toks = slice(g * LANES, (g + 1) * LANES)
        a_ref[toks, :] = jnp.transpose(oa_ref[rows, :])
        b_ref[toks, :] = jnp.transpose(ob_ref[rows, :])
        g_ref[toks, :] = jnp.transpose(og_ref[rows, :])


def _peerq_call(h1b, wq, keys):
    n, d = h1b.shape
    tb = SLAB_TOKENS
    slot = lambda dt: jax.ShapeDtypeStruct((n, PEER_SLOTS), dt)
    return pl.pallas_call(
        _peerq_kernel,
        grid=(n // tb,),
        in_specs=[pl.BlockSpec((tb, d), lambda i: (i, 0)),
                  pl.BlockSpec(wq.shape, lambda i: (0, 0)),
                  pl.BlockSpec(keys.shape, lambda i: (0, 0, 0))],
        out_specs=[pl.BlockSpec((tb, PEER_SLOTS), lambda i: (i, 0))] * 3,
        out_shape=[slot(jnp.int32), slot(jnp.int32), slot(F32)],
        scratch_shapes=[pltpu.VMEM((2 * PEER_HEADS * SLAB_ROWS, LANES), F32),
                        pltpu.VMEM((2 * PEER_HEADS * PEER_TOPK, SUBLANES, LANES), F32),
                        pltpu.VMEM((2 * PEER_HEADS * PEER_TOPK, SUBLANES, LANES), jnp.int32),
                        pltpu.VMEM((SLAB_ROWS, LANES), jnp.int32),
                        pltpu.VMEM((SLAB_ROWS, LANES), jnp.int32),
                        pltpu.VMEM((SLAB_ROWS, LANES), F32)],
        compiler_params=_params(("parallel",)),
        name="peer_retrieve",
    )(h1b, wq, keys)


def _peer_act_kernel(x_ref, u_ref, a_ref, b_ref, o_ref, *, ec):
    j = pl.program_id(1)

    @pl.when(j == 0)
    def _():
        o_ref[...] = jnp.zeros_like(o_ref)

    act = _dot_nt(x_ref[...], u_ref[...])
    a_idx = a_ref[...]
    b_idx = b_ref[...]
    acc = o_ref[...]
    groups = ec // PEER_N_KEYS
    for g in range(groups):
        i1 = j * groups + g
        picked = jnp.take_along_axis(act[:, g * PEER_N_KEYS:(g + 1) * PEER_N_KEYS], b_idx, axis=1)
        acc = jnp.where(a_idx == i1, picked, acc)
    o_ref[...] = acc


def _peer_act_call(h1b, u, a_idx, b_idx, tb, ec):
    n, d = h1b.shape
    return pl.pallas_call(
        functools.partial(_peer_act_kernel, ec=ec),
        grid=(n // tb, PEER_N_EXPERTS // ec),
        in_specs=[pl.BlockSpec((tb, d), lambda i, j: (i, 0)),
                  pl.BlockSpec((ec, d), lambda i, j: (j, 0)),
                  pl.BlockSpec((tb, PEER_SLOTS), lambda i, j: (i, 0)),
                  pl.BlockSpec((tb, PEER_SLOTS), lambda i, j: (i, 0))],
        out_specs=pl.BlockSpec((tb, PEER_SLOTS), lambda i, j: (i, 0)),
        out_shape=jax.ShapeDtypeStruct((n, PEER_SLOTS), F32),
        compiler_params=_params(("parallel", "arbitrary")),
        name="peer_expert_in",
    )(h1b, u, a_idx, b_idx)


def _peer_w_kernel(act_ref, g_ref, a_ref, b_ref, w_ref, wt_ref, w3_ref):
    tb = act_ref.shape[0]
    act = act_ref[...]
    gelu = 0.5 * act * (1.0 + lax.erf(act * (2.0 ** -0.5)))
    wt_ref[...] = g_ref[...] * gelu
    sub = lax.broadcasted_iota(jnp.int32, (PEER_N_KEYS, PEER_SLOTS), 0)

    def build_pair(w3_ref, g0, j):
        pts, qts = [], []
        for t in (g0 + j, g0 + j + 1):
            pts.append(jnp.where(sub == a_ref[t:t + 1, :], wt_ref[t:t + 1, :], 0.0).astype(BF16))
            qts.append(jnp.where(sub == b_ref[t:t + 1, :], 1.0, 0.0).astype(BF16))
        zero = jnp.zeros_like(qts[0])
        rhs_t = jnp.concatenate([jnp.concatenate([qts[0], zero], axis=1),
                                 jnp.concatenate([zero, qts[1]], axis=1)], axis=0)
        both = _dot_nt(jnp.concatenate(pts, axis=1), rhs_t)
        for k in range(2):
            r0 = (j + k) * W_ROW_PITCH
            w3_ref[r0:r0 + PEER_N_KEYS, :] = both[:, k * PEER_N_KEYS:(k + 1) * PEER_N_KEYS]

    def relayout(w3_ref, g0, i1):
        rows = w3_ref[pl.ds(i1, W_TOKEN_GROUP, stride=W_ROW_PITCH), :]
        w_ref[g0:g0 + W_TOKEN_GROUP, i1 * PEER_N_KEYS:(i1 + 1) * PEER_N_KEYS] = rows.astype(BF16)

    for g0 in range(0, tb, W_TOKEN_GROUP):
        for j in range(0, W_TOKEN_GROUP, 2):
            build_pair(w3_ref, g0, j)
        for i1 in range(PEER_N_KEYS):
            relayout(w3_ref, g0, i1)


def _peer_w_call(act, gate, a_idx, b_idx, tb):
    n = act.shape[0]
    spec = pl.BlockSpec((tb, PEER_SLOTS), lambda i: (i, 0))
    return pl.pallas_call(
        _peer_w_kernel,
        grid=(n // tb,),
        in_specs=[spec, spec, spec, spec],
        out_specs=pl.BlockSpec((tb, PEER_N_EXPERTS), lambda i: (i, 0)),
        out_shape=jax.ShapeDtypeStruct((n, PEER_N_EXPERTS), BF16),
        scratch_shapes=[pltpu.VMEM((tb, PEER_SLOTS), F32),
                        pltpu.VMEM((W_TOKEN_GROUP * W_ROW_PITCH, PEER_N_KEYS), F32)],
        compiler_params=_params(("parallel",)),
        name="peer_expert_weights",
    )(act, gate, a_idx, b_idx)


def _peer_out_kernel(w_ref, v_ref, h1_ref, h1b_ref, p_ref, gw_ref, pw_ref, g2_ref, b2_ref,
                     h2_ref, h2b_ref, acc_ref):
    j = pl.program_id(1)

    @pl.when(j == 0)
    def _():
        acc_ref[...] = jnp.zeros_like(acc_ref)

    acc_ref[...] += _dot(w_ref[...], v_ref[...])

    @pl.when(j == pl.num_programs(1) - 1)
    def _():
        gate = jax.nn.sigmoid(_dot(h1b_ref[...], gw_ref[...]))
        ple = gate * _dot(p_ref[...].astype(BF16), pw_ref[...])
        y = DEEPNORM_ALPHA * h1_ref[...] + acc_ref[...] + ple
        h2 = _layer_norm_rows(y, g2_ref[...], b2_ref[...])
        h2_ref[...] = h2
        h2b_ref[...] = h2.astype(BF16)


def _peer_out_call(w, v, h1, h1b, p, gw, pw, g2, b2, tm, tk):
    n, d = h1.shape
    row = lambda a: a.reshape(1, -1)
    return pl.pallas_call(
        _peer_out_kernel,
        grid=(n // tm, PEER_N_EXPERTS // tk),
        in_specs=[pl.BlockSpec((tm, tk), lambda i, j: (i, j)),
                  pl.BlockSpec((tk, d), lambda i, j: (j, 0)),
                  pl.BlockSpec((tm, d), lambda i, j: (i, 0)),
                  pl.BlockSpec((tm, d), lambda i, j: (i, 0)),
                  pl.BlockSpec((tm, PLE_DIM), lambda i, j: (i, 0)),
                  pl.BlockSpec(gw.shape, lambda i, j: (0, 0)),
                  pl.BlockSpec(pw.shape, lambda i, j: (0, 0)),
                  pl.BlockSpec((1, d), lambda i, j: (0, 0)),
                  pl.BlockSpec((1, d), lambda i, j: (0, 0))],
        out_specs=[pl.BlockSpec((tm, d), lambda i, j: (i, 0)),
                   pl.BlockSpec((tm, d), lambda i, j: (i, 0))],
        out_shape=[jax.ShapeDtypeStruct((n, d), F32), jax.ShapeDtypeStruct((n, d), BF16)],
        scratch_shapes=[pltpu.VMEM((tm, d), F32)],
        compiler_params=_params(("parallel", "arbitrary")),
        name="peer_out_ple_ln2",
    )(w, v, h1, h1b, p, gw, pw, row(g2), row(b2))


def _tiles(n, seq):
    pick = lambda pref, total: pref if total % pref == 0 else total
    return dict(
        rows=pick(512, n),
        tq=pick(512, seq), wide=pick(1024, seq),
        act_tb=pick(1024, n), act_ec=4096,
        w_tb=pick(128, n),
        out_tm=pick(512, n), out_tk=4096,
    )


def kernel(x, p, emb_ln_g, emb_ln_b, w_in, fox_fb, fox_norm_g, hg_lb_logits, hg_norm_g, w_out,
           ln1_g, ln1_b, peer_wq, peer_keys, peer_u, peer_v, ple_gate_w, ple_w, ln2_g, ln2_b):
    batch, seq, d = x.shape
    n = batch * seq
    assert n % SLAB_TOKENS == 0, "token count must be a multiple of the retrieval block"
    tl = _tiles(n, seq)
    h, hb = _ln_call(x.reshape(n, d), emb_ln_g, emb_ln_b, tl["rows"])
    scale = FOX_HEAD_DIM ** -0.5 * LOG2E
    o_ff = 3 * FOX_WIDTH
    o_hg = o_ff + FOX_HEADS
    for i in range(DEPTH):
        wi = w_in[i]
        wqkv = jnp.concatenate([wi[:, :FOX_WIDTH] * scale, wi[:, FOX_WIDTH:o_ff]], axis=1).astype(BF16)
        wff = jnp.pad(wi[:, o_ff:o_hg], ((0, 0), (0, LANES - FOX_HEADS))).astype(BF16)
        whg = wi[:, o_hg:].astype(BF16)
        qkv, ff, hgp = _inproj_call(hb, wqkv, wff, whg, tl["rows"])
        ff_t = ff[:, :FOX_HEADS].reshape(batch, seq, FOX_HEADS).transpose(0, 2, 1).reshape(batch * FOX_HEADS, seq)
        fb_rows = jnp.tile(fox_fb[i], batch).reshape(batch * FOX_HEADS, 1)
        c = _fox_cumsum_call(ff_t, fb_rows).reshape(batch, FOX_HEADS // 2, 2, seq)
        fox_o = _fox_call(qkv.reshape(batch, seq, 3 * FOX_WIDTH), c, batch, seq, tl["tq"], tl["wide"], FOX_PAIRS_PER_STEP)
        lbl = jnp.concatenate([hg_lb_logits.astype(F32), jnp.full((1, HG_WIDTH), float(i), F32)], axis=0)
        ohg = _hgrn_call(hgp.reshape(batch, seq, 4 * HG_WIDTH), lbl, batch, seq)
        h1, h1b = _mix_call(fox_o.reshape(n, FOX_WIDTH), ohg.reshape(n, HG_WIDTH), hgp, h,
                            fox_norm_g[i], hg_norm_g[i], w_out[i].astype(BF16), ln1_g[i], ln1_b[i], tl["rows"])
        keys = peer_keys[i].reshape(PEER_HEADS * 2, PEER_N_KEYS, PEER_HALF_DIM).astype(BF16)
        a_idx, b_idx, gate = _peerq_call(h1b, peer_wq[i].astype(BF16), keys)
        act = _peer_act_call(h1b, peer_u[i].astype(BF16), a_idx, b_idx, tl["act_tb"], tl["act_ec"])
        w = _peer_w_call(act, gate, a_idx, b_idx, tl["w_tb"])
        h, hb = _peer_out_call(w, peer_v[i].astype(BF16), h1, h1b, p[i].reshape(n, PLE_DIM),
                               ple_gate_w[i].astype(BF16), ple_w[i].astype(BF16), ln2_g[i], ln2_b[i],
                               tl["out_tm"], tl["out_tk"])
    return h.reshape(batch, seq, d)
```

```python
import functools

import jax
import jax.numpy as jnp
from jax import lax
from jax.experimental import pallas as pl
from jax.experimental.pallas import tpu as pltpu

D_MODEL = 1024
DEPTH = 2
PLE_DIM = 256
FOX_HEADS = 8
FOX_HEAD_DIM = 64
FOX_WIDTH = FOX_HEADS * FOX_HEAD_DIM
HG_HEADS = 4
HG_KEY_DIM = 128
HG_VAL_DIM = 128
HG_WIDTH = HG_HEADS * HG_VAL_DIM
PEER_HEADS = 8
PEER_N_KEYS = 128
PEER_N_EXPERTS = PEER_N_KEYS * PEER_N_KEYS
PEER_HALF_DIM = 128
PEER_TOPK = 16
PEER_SLOTS = PEER_HEADS * PEER_TOPK
DEEPNORM_ALPHA = (2 * DEPTH) ** 0.25
LN_EPS = 1e-5
RMS_EPS = 1e-6
MASK_VALUE = -1e30
LOG2E = 1.4426950408889634
LB_FLOOR = 1e-30

LANES = 128
SUBLANES = 8
VMEM_LIMIT_BYTES = 60 * 1024 * 1024

FOX_PAIRS_PER_STEP = 1
HG_CHUNK = 16
HG_CUMSUM_ROWS = 128
W_ROW_PITCH = PEER_N_KEYS + SUBLANES
W_TOKEN_GROUP = 16

F32 = jnp.float32
BF16 = jnp.bfloat16


def _params(sem):
    return pltpu.CompilerParams(dimension_semantics=sem, vmem_limit_bytes=VMEM_LIMIT_BYTES)


def _dot(a, b):
    return jnp.dot(a, b, preferred_element_type=F32)


def _dot_nt(a, b):
    return lax.dot_general(a, b, (((1,), (1,)), ((), ())), preferred_element_type=F32)


def _layer_norm_rows(y, g, b):
    mu = jnp.mean(y, axis=-1, keepdims=True)
    d = y - mu
    var = jnp.mean(d * d, axis=-1, keepdims=True)
    return d * lax.rsqrt(var + LN_EPS) * g + b


def _project(h, wqkv_ref, wff_ref, whg_ref, qkv_ref, ff_ref, hg_ref):
    qkv_ref[...] = _dot(h, wqkv_ref[...]).astype(BF16)
    ff_ref[...] = _dot(h, wff_ref[...])
    hg_ref[...] = _dot(h, whg_ref[...])


def _inproj_kernel(h_ref, *refs):
    _project(h_ref[...], *refs)


def _ln_inproj_kernel(x_ref, g_ref, b_ref, wqkv_ref, wff_ref, whg_ref, h_ref, *out_refs):
    h = _layer_norm_rows(x_ref[...], g_ref[...], b_ref[...])
    h_ref[...] = h
    _project(h.astype(BF16), wqkv_ref, wff_ref, whg_ref, *out_refs)


def _inproj_call(src, wqkv, wff, whg, tm, ln=None):
    n, d = src.shape
    rows = lambda width: pl.BlockSpec((tm, width), lambda i: (i, 0))
    whole = lambda a: pl.BlockSpec(a.shape, lambda i: (0,) * a.ndim)
    weights = (wqkv, wff, whg)
    outs = [(w.shape[1], dt) for w, dt in zip(weights, (BF16, F32, F32))]
    args, in_specs = [src], [rows(d)]
    if ln is not None:
        ln = [a.reshape(1, d) for a in ln]
        args += ln
        in_specs += [whole(a) for a in ln]
        outs = [(d, F32)] + outs
    return pl.pallas_call(
        _inproj_kernel if ln is None else _ln_inproj_kernel,
        grid=(n // tm,),
        in_specs=in_specs + [whole(w) for w in weights],
        out_specs=[rows(width) for width, _ in outs],
        out_shape=[jax.ShapeDtypeStruct((n, width), dt) for width, dt in outs],
        compiler_params=_params(("parallel",)),
        name="in_proj",
    )(*args, *weights)


def _log_sigmoid(z):
    return jnp.minimum(z, 0.0) - jnp.log(1.0 + jnp.exp(-jnp.abs(z)))


def _fox_cumsum_kernel(ff_ref, fb_ref, c_ref):
    rows, t = ff_ref.shape
    lf = _log_sigmoid(ff_ref[...] + fb_ref[...]) * LOG2E
    r = lax.broadcasted_iota(jnp.int32, (LANES, LANES), 0)
    c = lax.broadcasted_iota(jnp.int32, (LANES, LANES), 1)
    upper = (r <= c).astype(F32).astype(BF16)
    carry = jnp.zeros((rows, 1), F32)
    for j in range(t // LANES):
        blk = lf[:, j * LANES:(j + 1) * LANES]
        cs = sum(_dot(piece.astype(BF16), upper) for piece in _split3(blk))
        cs = cs + carry
        c_ref[:, j * LANES:(j + 1) * LANES] = cs
        carry = cs[:, LANES - 1:LANES]


def _fox_cumsum_call(ff_t, fb_rows):
    rows, t = ff_t.shape
    tr = SUBLANES
    return pl.pallas_call(
        _fox_cumsum_kernel,
        grid=(rows // tr,),
        in_specs=[pl.BlockSpec((tr, t), lambda i: (i, 0)),
                  pl.BlockSpec((tr, 1), lambda i: (i, 0))],
        out_specs=pl.BlockSpec((tr, t), lambda i: (i, 0)),
        out_shape=jax.ShapeDtypeStruct((rows, t), F32),
        compiler_params=_params(("parallel",)),
        name="fox_cumsum",
    )(ff_t, fb_rows)


def _split3(x):
    hi = x.astype(BF16).astype(F32)
    r1 = x - hi
    mid = r1.astype(BF16).astype(F32)
    lo = (r1 - mid).astype(BF16).astype(F32)
    return hi, mid, lo


def _fox_kernel(q_ref, k_ref, v_ref, c_ref, o_ref, kx_ref, vx_ref, qe_ref, *, tq, wide, seq, pairs):
    qi = pl.program_id(2)
    lane = lax.broadcasted_iota(jnp.int32, (1, LANES), 1)
    data = [lane < FOX_HEAD_DIM, lane >= FOX_HEAD_DIM]
    spare = [FOX_HEAD_DIM, 0]
    heads = [(p, h) for p in range(pairs) for h in range(2)]
    slab = lambda p: slice(p * LANES, (p + 1) * LANES)

    sub8 = lax.broadcasted_iota(jnp.int32, (SUBLANES, LANES), 0)

    def bias_tile(h, pieces):
        rows8 = jnp.zeros((SUBLANES, LANES), F32)
        for j, piece in enumerate(pieces):
            rows8 = jnp.where(sub8 == j, piece, rows8)
        blocks = [jnp.zeros((spare[h], LANES), F32)] if spare[h] else []
        blocks += [rows8, jnp.zeros((LANES - spare[h] - SUBLANES, LANES), F32)]
        return jnp.transpose(jnp.concatenate(blocks, axis=0))

    @pl.when(qi == 0)
    def _():
        for n, (p, h) in enumerate(heads):
            for j in range(seq // LANES):
                rows = slice(j * LANES, (j + 1) * LANES)
                hi, mid, lo = _split3(c_ref[p, h:h + 1, rows])
                one = jnp.ones_like(hi)
                k_extra = bias_tile(h, (-hi, -mid, -lo, one, one, one))
                kx_ref[n, rows, :] = jnp.where(data[h], k_ref[rows, slab(p)].astype(F32), k_extra).astype(BF16)
                qe_ref[n, rows, :] = bias_tile(h, (one, one, one, hi, mid, lo)).astype(BF16)
                vx_ref[n, rows, :] = jnp.where(data[h], v_ref[rows, slab(p)].astype(F32), 1.0).astype(BF16)

    q0 = pl.multiple_of(qi * tq, tq)
    qx = []
    for n, (p, h) in enumerate(heads):
        q = q_ref[:, slab(p)].astype(F32)
        q_extra = qe_ref[n, pl.ds(q0, tq), :].astype(F32)
        qx.append((jnp.where(data[h], q, 0.0) + q_extra).astype(BF16))

    def block(k0, width, masked, carry):
        out = []
        for n in range(len(heads)):
            m_old, acc_old = carry[n]
            s = _dot_nt(qx[n], kx_ref[n, pl.ds(k0, width), :])
            if masked:
                row = lax.broadcasted_iota(jnp.int32, (tq, width), 0) + q0
                col = lax.broadcasted_iota(jnp.int32, (tq, width), 1) + k0
                s = jnp.where(col <= row, s, MASK_VALUE)
            m_new = jnp.maximum(m_old, jnp.max(s, axis=-1, keepdims=True))
            alpha = jnp.exp2(m_old - m_new)
            p = jnp.exp2(s - m_new).astype(BF16)
            acc_new = alpha * acc_old + _dot(p, vx_ref[n, pl.ds(k0, width), :])
            out.append((m_new, acc_new))
        return tuple(out)

    init = tuple((jnp.full((tq, 1), MASK_VALUE, F32), jnp.zeros((tq, LANES), F32)) for _ in heads)
    assert wide in (tq, 2 * tq)
    n_wide = q0 // wide
    carry = lax.fori_loop(
        0, n_wide, lambda j, c: block(pl.multiple_of(j * wide, wide), wide, False, c), init)
    if wide == tq:
        carry = block(q0, tq, True, carry)
    else:
        carry = lax.cond(q0 % wide == 0,
                         lambda c: block(q0, tq, True, c),
                         lambda c: block(pl.multiple_of(n_wide * wide, wide), wide, True, c),
                         carry)
    for p in range(pairs):
        res = []
        for h in range(2):
            acc = carry[2 * p + h][1]
            res.append(acc / pltpu.roll(acc, FOX_HEAD_DIM, axis=1))
        o_ref[:, slab(p)] = jnp.where(data[0], res[0], res[1]).astype(o_ref.dtype)


def _fox_call(qkv, c, batch, seq, tq, wide, pairs):
    steps = FOX_HEADS // 2 // pairs
    width = pairs * LANES
    return pl.pallas_call(
        functools.partial(_fox_kernel, tq=tq, wide=wide, seq=seq, pairs=pairs),
        grid=(batch, steps, seq // tq),
        in_specs=[pl.BlockSpec((None, tq, width), lambda b, p, i: (b, i, p)),
                  pl.BlockSpec((None, seq, width), lambda b, p, i: (b, 0, steps + p)),
                  pl.BlockSpec((None, seq, width), lambda b, p, i: (b, 0, 2 * steps + p)),
                  pl.BlockSpec((None, pairs, 2, seq), lambda b, p, i: (b, p, 0, 0))],
        out_specs=pl.BlockSpec((None, tq, width), lambda b, p, i: (b, i, p)),
        out_shape=jax.ShapeDtypeStruct((batch, seq, FOX_WIDTH), BF16),
        scratch_shapes=[pltpu.VMEM((2 * pairs, seq, LANES), BF16),
                        pltpu.VMEM((2 * pairs, seq, LANES), BF16),
                        pltpu.VMEM((2 * pairs, seq, LANES), BF16)],
        compiler_params=_params(("parallel", "parallel", "arbitrary")),
        name="fox_attention",
    )(qkv, qkv, qkv, c)


def _hgrn_kernel(q_ref, z_ref, v_ref, lbl_ref, o_ref, lf_ref, kk_ref, st_ref, *, seq):
    logits = lbl_ref[...]
    lg = logits[:DEPTH, :]
    mx = jnp.max(lg, axis=0, keepdims=True)
    ex = jnp.exp(lg - mx)
    probs = ex / jnp.sum(ex, axis=0, keepdims=True)
    sel = logits[DEPTH:DEPTH + 1, :]
    lb = jnp.zeros_like(sel)
    run = jnp.zeros_like(sel)
    for i in range(DEPTH):
        run = run + probs[i:i + 1, :]
        lb = jnp.where(sel == float(i), run - probs[0:1, :], lb)
    log_lb = jnp.log(jnp.maximum(lb, LB_FLOOR))
    log_1m = jnp.log(1.0 - lb)

    rr = lax.broadcasted_iota(jnp.int32, (HG_CUMSUM_ROWS, HG_CUMSUM_ROWS), 0)
    cc = lax.broadcasted_iota(jnp.int32, (HG_CUMSUM_ROWS, HG_CUMSUM_ROWS), 1)
    tri = ((rr // HG_CHUNK == cc // HG_CHUNK) & (cc <= rr)).astype(F32).astype(BF16)

    def prep(j, _):
        r0 = pl.multiple_of(j * HG_CUMSUM_ROWS, HG_CUMSUM_ROWS)
        z = z_ref[pl.ds(r0, HG_CUMSUM_ROWS), :]
        a = log_1m + _log_sigmoid(z)
        mm = jnp.maximum(log_lb, a)
        lf = (mm + jnp.log(1.0 + jnp.exp(-jnp.abs(log_lb - a)))) * LOG2E
        lf_ref[pl.ds(r0, HG_CUMSUM_ROWS), :] = sum(_dot(tri, piece.astype(BF16)) for piece in _split3(lf))
        kk_ref[pl.ds(r0, HG_CUMSUM_ROWS), :] = (1.0 - lb) * jax.nn.sigmoid(-z)
        return 0

    lax.fori_loop(0, seq // HG_CUMSUM_ROWS, prep, 0, unroll=2)
    st_ref[...] = jnp.zeros_like(st_ref)
    row8 = lax.broadcasted_iota(jnp.int32, (SUBLANES, 1), 0)

    def chunk(ci, _):
        r0 = pl.multiple_of(ci * HG_CHUNK, HG_CHUNK)
        b = lf_ref[pl.ds(r0, HG_CHUNK), :]
        q = q_ref[pl.ds(r0, HG_CHUNK), :]
        kk = kk_ref[pl.ds(r0, HG_CHUNK), :]
        v = v_ref[pl.ds(r0, HG_CHUNK), :]
        b_last = b[HG_CHUNK - 1:HG_CHUNK, :]
        qe = (q * jnp.exp2(b)).astype(BF16)
        kd = (kk * jnp.exp2(b_last - b)).astype(BF16)
        e_last = jnp.exp2(b_last)
        outs = []
        for h in range(HG_HEADS):
            sl = slice(h * HG_KEY_DIM, (h + 1) * HG_KEY_DIM)
            st = st_ref[h]
            o = _dot_nt(qe[:, sl], st.astype(BF16))
            bh, qh, kh, vh = b[:, sl], q[:, sl], kk[:, sl], v[:, sl]
            halves = []
            for half in range(HG_CHUNK // SUBLANES):
                rows = slice(half * SUBLANES, (half + 1) * SUBLANES)
                bt, qt, acc = bh[rows], qh[rows], o[rows]
                for s in range((half + 1) * SUBLANES):
                    diff = bt - bh[s:s + 1, :]
                    first_live = s - half * SUBLANES
                    if first_live > 0:
                        diff = jnp.where(row8 >= first_live, diff, MASK_VALUE)
                    a_col = jnp.sum(qt * kh[s:s + 1, :] * jnp.exp2(diff), axis=-1, keepdims=True)
                    acc = acc + a_col * vh[s:s + 1, :]
                halves.append(acc)
            o = jnp.concatenate(halves, axis=0)
            upd = lax.dot_general(vh.astype(BF16), kd[:, sl], (((0,), (0,)), ((), ())),
                                  preferred_element_type=F32)
            st_ref[h] = st * e_last[:, sl] + upd
            outs.append(o)
        o_ref[pl.ds(r0, HG_CHUNK), :] = jnp.concatenate(outs, axis=1).astype(o_ref.dtype)
        return 0

    lax.fori_loop(0, seq // HG_CHUNK, chunk, 0, unroll=4)


def _hgrn_call(hgp, lbl, batch, seq):
    return pl.pallas_call(
        functools.partial(_hgrn_kernel, seq=seq),
        grid=(batch,),
        in_specs=[pl.BlockSpec((None, seq, HG_WIDTH), lambda b: (b, 0, 0)),
                  pl.BlockSpec((None, seq, HG_WIDTH), lambda b: (b, 0, 1)),
                  pl.BlockSpec((None, seq, HG_WIDTH), lambda b: (b, 0, 2)),
                  pl.BlockSpec(lbl.shape, lambda b: (0, 0))],
        out_specs=pl.BlockSpec((None, seq, HG_WIDTH), lambda b: (b, 0, 0)),
        out_shape=jax.ShapeDtypeStruct((batch, seq, HG_WIDTH), BF16),
        scratch_shapes=[pltpu.VMEM((seq, HG_WIDTH), F32),
                        pltpu.VMEM((seq, HG_WIDTH), F32),
                        pltpu.VMEM((HG_HEADS, HG_VAL_DIM, HG_KEY_DIM), F32)],
        compiler_params=_params(("parallel",)),
        name="hgrn2",
    )(hgp, hgp, hgp, lbl)


def _mix_kernel(fox_ref, ohg_ref, gate_ref, h_ref, gf_ref, gh_ref, wo_ref, g1_ref, b1_ref,
                h1_ref, h1b_ref):
    fox = fox_ref[...].astype(F32)
    fox = fox * lax.rsqrt(jnp.mean(fox * fox, axis=-1, keepdims=True) + RMS_EPS) * gf_ref[...]
    ohg = ohg_ref[...].astype(F32)
    gh = gh_ref[...]
    gate = gate_ref[...]
    parts = [fox.astype(BF16)]
    for h in range(HG_HEADS):
        sl = slice(h * HG_VAL_DIM, (h + 1) * HG_VAL_DIM)
        o = ohg[:, sl]
        o = o * lax.rsqrt(jnp.mean(o * o, axis=-1, keepdims=True) + RMS_EPS) * gh[:, sl]
        g = gate[:, sl]
        parts.append((o * (g * jax.nn.sigmoid(g))).astype(BF16))
    cat = jnp.concatenate(parts, axis=1)
    mix = _dot(cat, wo_ref[...])
    h1 = _layer_norm_rows(DEEPNORM_ALPHA * h_ref[...] + mix, g1_ref[...], b1_ref[...])
    h1_ref[...] = h1
    h1b_ref[...] = h1.astype(BF16)


def _mix_call(fox_o, ohg, hgp, h, gf, gh, wo, g1, b1, tm):
    n, d = h.shape
    row = lambda a: a.reshape(1, -1)
    return pl.pallas_call(
        _mix_kernel,
        grid=(n // tm,),
        in_specs=[pl.BlockSpec((tm, FOX_WIDTH), lambda i: (i, 0)),
                  pl.BlockSpec((tm, HG_WIDTH), lambda i: (i, 0)),
                  pl.BlockSpec((tm, HG_WIDTH), lambda i: (i, 3)),
                  pl.BlockSpec((tm, d), lambda i: (i, 0)),
                  pl.BlockSpec((1, FOX_WIDTH), lambda i: (0, 0)),
                  pl.BlockSpec((1, HG_WIDTH), lambda i: (0, 0)),
                  pl.BlockSpec(wo.shape, lambda i: (0, 0)),
                  pl.BlockSpec((1, d), lambda i: (0, 0)),
                  pl.BlockSpec((1, d), lambda i: (0, 0))],
        out_specs=[pl.BlockSpec((tm, d), lambda i: (i, 0)),
                   pl.BlockSpec((tm, d), lambda i: (i, 0))],
        out_shape=[jax.ShapeDtypeStruct((n, d), F32), jax.ShapeDtypeStruct((n, d), BF16)],
        compiler_params=_params(("parallel",)),
        name="mix_out_ln1",
    )(fox_o, ohg, hgp, h, row(gf), row(gh), wo, row(g1), row(b1))


class _Network:
    def __init__(self):
        self.comparators = []
        self.outputs = []


def _net_merge(net, xs, ys):
    if not xs:
        return list(ys)
    if not ys:
        return list(xs)
    if len(xs) == 1 and len(ys) == 1:
        net.comparators.append((xs[0], ys[0]))
        return [xs[0], ys[0]]
    evens = _net_merge(net, xs[0::2], ys[0::2])
    odds = _net_merge(net, xs[1::2], ys[1::2])
    out = [evens[0]]
    for i, w in enumerate(odds):
        if i + 1 < len(evens):
            net.comparators.append((evens[i + 1], w))
            out += [evens[i + 1], w]
        else:
            out.append(w)
    return out + evens[len(odds) + 1:]


def _net_sort(net, ws):
    if len(ws) <= 1:
        return list(ws)
    mid = len(ws) // 2
    return _net_merge(net, _net_sort(net, ws[:mid]), _net_sort(net, ws[mid:]))


def _top_network(sizes, presorted, keep):
    net = _Network()
    lists, w = [], 0
    for s in sizes:
        ws = list(range(w, w + s))
        w += s
        lists.append(ws if presorted else _net_sort(net, ws))
    while len(lists) > 1:
        lists.sort(key=len)
        lists = [_net_merge(net, lists[0], lists[1])[:keep]] + lists[2:]
    net.outputs = lists[0][:keep]
    needed, kept = set(net.outputs), []
    for a, b in reversed(net.comparators):
        if a in needed or b in needed:
            kept.append((a, b, a in needed, b in needed))
            needed.update((a, b))
    net.comparators = kept[::-1]
    return net


def _run_network(net, wires):
    wires = list(wires)
    for a, b, need_a, need_b in net.comparators:
        (va, ia), (vb, ib) = wires[a], wires[b]
        a_first = (va > vb) | ((va == vb) & (ia < ib))
        if need_a:
            wires[a] = (jnp.maximum(va, vb), jnp.where(a_first, ia, ib))
        if need_b:
            wires[b] = (jnp.minimum(va, vb), jnp.where(a_first, ib, ia))
    return [wires[w] for w in net.outputs]


def _pair_lists():
    ok = lambda j1, j2: (j1 + 1) * (j2 + 1) <= PEER_TOPK
    lists = []
    for d in range(PEER_TOPK):
        row = [(d, j2) for j2 in range(d, PEER_TOPK) if ok(d, j2)]
        col = [(j1, d) for j1 in range(d + 1, PEER_TOPK) if ok(j1, d)]
        lists += [l for l in (row, col) if l]
    return lists


KEY_BITS = PEER_N_KEYS.bit_length() - 1
FLAT_SHIFT = 2 * KEY_BITS
_KEY_NET = _top_network([PEER_TOPK] * (PEER_N_KEYS // PEER_TOPK), presorted=False, keep=PEER_TOPK)
_PAIR_LISTS = _pair_lists()
_PAIR_NET = _top_network([len(l) for l in _PAIR_LISTS], presorted=True, keep=PEER_TOPK)


SLAB_TOKENS = SUBLANES * LANES
SLAB_ROWS = SUBLANES * W_ROW_PITCH


def _peerq_kernel(h_ref, wq_ref, keys_ref, a_ref, b_ref, g_ref,
                  sc_ref, tv_ref, ti_ref, oa_ref, ob_ref, og_ref):
    n_lists = 2 * PEER_HEADS
    q = _dot(h_ref[...], wq_ref[...]).astype(BF16)
    for hp in range(n_lists):
        s = _dot_nt(keys_ref[hp], q[:, hp * PEER_HALF_DIM:(hp + 1) * PEER_HALF_DIM])
        for g in range(SUBLANES):
            sc_ref[pl.ds(hp * SLAB_ROWS + g * W_ROW_PITCH, PEER_N_KEYS), :] = s[:, g * LANES:(g + 1) * LANES]

    def keys_top(hp, _):
        base = hp * SLAB_ROWS
        wires = [(sc_ref[pl.ds(base + n, SUBLANES, stride=W_ROW_PITCH), :],
                  jnp.full((SUBLANES, LANES), n, jnp.int32)) for n in range(PEER_N_KEYS)]
        for j, (v, i) in enumerate(_run_network(_KEY_NET, wires)):
            tv_ref[hp * PEER_TOPK + j] = v
            ti_ref[hp * PEER_TOPK + j] = i
        return 0

    lax.fori_loop(0, n_lists, keys_top, 0)

    def pairs_top(h, _):
        r1 = 2 * h * PEER_TOPK
        r2 = r1 + PEER_TOPK
        s1 = [tv_ref[r1 + j] for j in range(PEER_TOPK)]
        s2 = [tv_ref[r2 + j] for j in range(PEER_TOPK)]
        e1 = [ti_ref[r1 + j] * PEER_N_KEYS for j in range(PEER_TOPK)]
        i2 = [ti_ref[r2 + j] for j in range(PEER_TOPK)]
        wires = [(s1[j1] + s2[j2], e1[j1] + i2[j2] + ((j1 * PEER_TOPK + j2) << FLAT_SHIFT))
                 for lst in _PAIR_LISTS for (j1, j2) in lst]
        top = _run_network(_PAIR_NET, wires)
        es = [jnp.exp(v - top[0][0]) for v, _ in top]
        total = es[0]
        for e in es[1:]:
            total = total + e
        inv = 1.0 / total
        for k, ((_, key), e) in enumerate(zip(top, es)):
            expert = key & ((1 << FLAT_SHIFT) - 1)
            rows = pl.ds(h * PEER_TOPK + k, SUBLANES, stride=W_ROW_PITCH)
            oa_ref[rows, :] = lax.shift_right_logical(expert, KEY_BITS)
            ob_ref[rows, :] = expert & (PEER_N_KEYS - 1)
            og_ref[rows, :] = e * inv
        return 0

    lax.fori_loop(0, PEER_HEADS, pairs_top, 0)
    for g in range(SUBLANES):
        rows = pl.ds(g * W_ROW_PITCH, PEER_SLOTS)
        toks = slice(g * LANES, (g + 1) * LANES)
        a_ref[toks, :] = jnp.transpose(oa_ref[rows, :])
        b_ref[toks, :] = jnp.transpose(ob_ref[rows, :])
        g_ref[toks, :] = jnp.transpose(og_ref[rows, :])


def _peerq_call(h1b, wq, keys):
    n, d = h1b.shape
    tb = SLAB_TOKENS
    slot = lambda dt: jax.ShapeDtypeStruct((n, PEER_SLOTS), dt)
    return pl.pallas_call(
        _peerq_kernel,
        grid=(n // tb,),
        in_specs=[pl.BlockSpec((tb, d), lambda i: (i, 0)),
                  pl.BlockSpec(wq.shape, lambda i: (0, 0)),
                  pl.BlockSpec(keys.shape, lambda i: (0, 0, 0))],
        out_specs=[pl.BlockSpec((tb, PEER_SLOTS), lambda i: (i, 0))] * 3,
        out_shape=[slot(jnp.int32), slot(jnp.int32), slot(F32)],
        scratch_shapes=[pltpu.VMEM((2 * PEER_HEADS * SLAB_ROWS, LANES), F32),
                        pltpu.VMEM((2 * PEER_HEADS * PEER_TOPK, SUBLANES, LANES), F32),
                        pltpu.VMEM((2 * PEER_HEADS * PEER_TOPK, SUBLANES, LANES), jnp.int32),
                        pltpu.VMEM((SLAB_ROWS, LANES), jnp.int32),
                        pltpu.VMEM((SLAB_ROWS, LANES), jnp.int32),
                        pltpu.VMEM((SLAB_ROWS, LANES), F32)],
        compiler_params=_params(("parallel",)),
        name="peer_retrieve",
    )(h1b, wq, keys)


def _peer_act_kernel(x_ref, u_ref, a_ref, b_ref, o_ref, *, ec):
    j = pl.program_id(1)

    @pl.when(j == 0)
    def _():
        o_ref[...] = jnp.zeros_like(o_ref)

    act = _dot_nt(x_ref[...], u_ref[...])
    a_idx = a_ref[...]
    b_idx = b_ref[...]
    acc = o_ref[...]
    groups = ec // PEER_N_KEYS
    for g in range(groups):
        i1 = j * groups + g
        picked = jnp.take_along_axis(act[:, g * PEER_N_KEYS:(g + 1) * PEER_N_KEYS], b_idx, axis=1)
        acc = jnp.where(a_idx == i1, picked, acc)
    o_ref[...] = acc


def _peer_act_call(h1b, u, a_idx, b_idx, tb, ec):
    n, d = h1b.shape
    return pl.pallas_call(
        functools.partial(_peer_act_kernel, ec=ec),
        grid=(n // tb, PEER_N_EXPERTS // ec),
        in_specs=[pl.BlockSpec((tb, d), lambda i, j: (i, 0)),
                  pl.BlockSpec((ec, d), lambda i, j: (j, 0)),
                  pl.BlockSpec((tb, PEER_SLOTS), lambda i, j: (i, 0)),
                  pl.BlockSpec((tb, PEER_SLOTS), lambda i, j: (i, 0))],
        out_specs=pl.BlockSpec((tb, PEER_SLOTS), lambda i, j: (i, 0)),
        out_shape=jax.ShapeDtypeStruct((n, PEER_SLOTS), F32),
        compiler_params=_params(("parallel", "arbitrary")),
        name="peer_expert_in",
    )(h1b, u, a_idx, b_idx)


def _peer_w_kernel(act_ref, g_ref, a_ref, b_ref, w_ref, wt_ref, w3_ref):
    tb = act_ref.shape[0]
    act = act_ref[...]
    gelu = 0.5 * act * (1.0 + lax.erf(act * (2.0 ** -0.5)))
    wt_ref[...] = g_ref[...] * gelu
    sub = lax.broadcasted_iota(jnp.int32, (PEER_N_KEYS, PEER_SLOTS), 0)

    def build_pair(w3_ref, g0, j):
        pts, qts = [], []
        for t in (g0 + j, g0 + j + 1):
            pts.append(jnp.where(sub == a_ref[t:t + 1, :], wt_ref[t:t + 1, :], 0.0).astype(BF16))
            qts.append(jnp.where(sub == b_ref[t:t + 1, :], 1.0, 0.0).astype(BF16))
        zero = jnp.zeros_like(qts[0])
        rhs_t = jnp.concatenate([jnp.concatenate([qts[0], zero], axis=1),
                                 jnp.concatenate([zero, qts[1]], axis=1)], axis=0)
        both = _dot_nt(jnp.concatenate(pts, axis=1), rhs_t)
        for k in range(2):
            r0 = (j + k) * W_ROW_PITCH
            w3_ref[r0:r0 + PEER_N_KEYS, :] = both[:, k * PEER_N_KEYS:(k + 1) * PEER_N_KEYS]

    def relayout(w3_ref, g0, i1):
        rows = w3_ref[pl.ds(i1, W_TOKEN_GROUP, stride=W_ROW_PITCH), :]
        w_ref[g0:g0 + W_TOKEN_GROUP, i1 * PEER_N_KEYS:(i1 + 1) * PEER_N_KEYS] = rows.astype(BF16)

    for g0 in range(0, tb, W_TOKEN_GROUP):
        for j in range(0, W_TOKEN_GROUP, 2):
            build_pair(w3_ref, g0, j)
        for i1 in range(PEER_N_KEYS):
            relayout(w3_ref, g0, i1)


def _peer_w_call(act, gate, a_idx, b_idx, tb):
    n = act.shape[0]
    spec = pl.BlockSpec((tb, PEER_SLOTS), lambda i: (i, 0))
    return pl.pallas_call(
        _peer_w_kernel,
        grid=(n // tb,),
        in_specs=[spec, spec, spec, spec],
        out_specs=pl.BlockSpec((tb, PEER_N_EXPERTS), lambda i: (i, 0)),
        out_shape=jax.ShapeDtypeStruct((n, PEER_N_EXPERTS), BF16),
        scratch_shapes=[pltpu.VMEM((tb, PEER_SLOTS), F32),
                        pltpu.VMEM((W_TOKEN_GROUP * W_ROW_PITCH, PEER_N_KEYS), F32)],
        compiler_params=_params(("parallel",)),
        name="peer_expert_weights",
    )(act, gate, a_idx, b_idx)


def _peer_out_kernel(w_ref, v_ref, h1_ref, h1b_ref, p_ref, gw_ref, pw_ref, g2_ref, b2_ref,
                     h2_ref, h2b_ref, acc_ref):
    j = pl.program_id(1)

    @pl.when(j == 0)
    def _():
        acc_ref[...] = jnp.zeros_like(acc_ref)

    acc_ref[...] += _dot(w_ref[...], v_ref[...])

    @pl.when(j == pl.num_programs(1) - 1)
    def _():
        gate = jax.nn.sigmoid(_dot(h1b_ref[...], gw_ref[...]))
        ple = gate * _dot(p_ref[...].astype(BF16), pw_ref[...])
        y = DEEPNORM_ALPHA * h1_ref[...] + acc_ref[...] + ple
        h2 = _layer_norm_rows(y, g2_ref[...], b2_ref[...])
        h2_ref[...] = h2
        h2b_ref[...] = h2.astype(BF16)


def _peer_out_call(w, v, h1, h1b, p, gw, pw, g2, b2, tm, tk):
    n, d = h1.shape
    row = lambda a: a.reshape(1, -1)
    return pl.pallas_call(
        _peer_out_kernel,
        grid=(n // tm, PEER_N_EXPERTS // tk),
        in_specs=[pl.BlockSpec((tm, tk), lambda i, j: (i, j)),
                  pl.BlockSpec((tk, d), lambda i, j: (j, 0)),
                  pl.BlockSpec((tm, d), lambda i, j: (i, 0)),
                  pl.BlockSpec((tm, d), lambda i, j: (i, 0)),
                  pl.BlockSpec((tm, PLE_DIM), lambda i, j: (i, 0)),
                  pl.BlockSpec(gw.shape, lambda i, j: (0, 0)),
                  pl.BlockSpec(pw.shape, lambda i, j: (0, 0)),
                  pl.BlockSpec((1, d), lambda i, j: (0, 0)),
                  pl.BlockSpec((1, d), lambda i, j: (0, 0))],
        out_specs=[pl.BlockSpec((tm, d), lambda i, j: (i, 0)),
                   pl.BlockSpec((tm, d), lambda i, j: (i, 0))],
        out_shape=[jax.ShapeDtypeStruct((n, d), F32), jax.ShapeDtypeStruct((n, d), BF16)],
        scratch_shapes=[pltpu.VMEM((tm, d), F32)],
        compiler_params=_params(("parallel", "arbitrary")),
        name="peer_out_ple_ln2",
    )(w, v, h1, h1b, p, gw, pw, row(g2), row(b2))


def _tiles(n, seq):
    pick = lambda pref, total: pref if total % pref == 0 else total
    return dict(
        rows=pick(512, n),
        tq=pick(512, seq), wide=pick(1024, seq),
        act_tb=pick(1024, n), act_ec=4096,
        w_tb=pick(128, n),
        out_tm=pick(1024, n), out_tk=2048,
    )


def kernel(x, p, emb_ln_g, emb_ln_b, w_in, fox_fb, fox_norm_g, hg_lb_logits, hg_norm_g, w_out,
           ln1_g, ln1_b, peer_wq, peer_keys, peer_u, peer_v, ple_gate_w, ple_w, ln2_g, ln2_b):
    batch, seq, d = x.shape
    n = batch * seq
    assert n % SLAB_TOKENS == 0, "token count must be a multiple of the retrieval block"
    tl = _tiles(n, seq)
    scale = FOX_HEAD_DIM ** -0.5 * LOG2E
    o_ff = 3 * FOX_WIDTH
    o_hg = o_ff + FOX_HEADS
    for i in range(DEPTH):
        wi = w_in[i]
        wqkv = jnp.concatenate([wi[:, :FOX_WIDTH] * scale, wi[:, FOX_WIDTH:o_ff]], axis=1).astype(BF16)
        wff = jnp.pad(wi[:, o_ff:o_hg], ((0, 0), (0, LANES - FOX_HEADS))).astype(BF16)
        whg = wi[:, o_hg:].astype(BF16)
        if i == 0:
            h, qkv, ff, hgp = _inproj_call(x.reshape(n, d), wqkv, wff, whg, tl["rows"],
                                           ln=(emb_ln_g, emb_ln_b))
        else:
            qkv, ff, hgp = _inproj_call(hb, wqkv, wff, whg, tl["rows"])
        ff_t = ff[:, :FOX_HEADS].reshape(batch, seq, FOX_HEADS).transpose(0, 2, 1).reshape(batch * FOX_HEADS, seq)
        fb_rows = jnp.tile(fox_fb[i], batch).reshape(batch * FOX_HEADS, 1)
        c = _fox_cumsum_call(ff_t, fb_rows).reshape(batch, FOX_HEADS // 2, 2, seq)
        fox_o = _fox_call(qkv.reshape(batch, seq, 3 * FOX_WIDTH), c, batch, seq, tl["tq"], tl["wide"], FOX_PAIRS_PER_STEP)
        lbl = jnp.concatenate([hg_lb_logits.astype(F32), jnp.full((1, HG_WIDTH), float(i), F32)], axis=0)
        ohg = _hgrn_call(hgp.reshape(batch, seq, 4 * HG_WIDTH), lbl, batch, seq)
        h1, h1b = _mix_call(fox_o.reshape(n, FOX_WIDTH), ohg.reshape(n, HG_WIDTH), hgp, h,
                            fox_norm_g[i], hg_norm_g[i], w_out[i].astype(BF16), ln1_g[i], ln1_b[i], tl["rows"])
        keys = peer_keys[i].reshape(PEER_HEADS * 2, PEER_N_KEYS, PEER_HALF_DIM).astype(BF16)
        a_idx, b_idx, gate = _peerq_call(h1b, peer_wq[i].astype(BF16), keys)
        act = _peer_act_call(h1b, peer_u[i].astype(BF16), a_idx, b_idx, tl["act_tb"], tl["act_ec"])
        w = _peer_w_call(act, gate, a_idx, b_idx, tl["w_tb"])
        h, hb = _peer_out_call(w, peer_v[i].astype(BF16), h1, h1b, p[i].reshape(n, PLE_DIM),
                               ple_gate_w[i].astype(BF16), ple_w[i].astype(BF16), ln2_g[i], ln2_b[i],
                               tl["out_tm"], tl["out_tk"])
    return h.reshape(batch, seq, d)
```

```python
import functools

import jax
import jax.numpy as jnp
from jax import lax
from jax.experimental import pallas as pl
from jax.experimental.pallas import tpu as pltpu

D_MODEL = 1024
DEPTH = 2
PLE_DIM = 256
FOX_HEADS = 8
FOX_HEAD_DIM = 64
FOX_WIDTH = FOX_HEADS * FOX_HEAD_DIM
HG_HEADS = 4
HG_KEY_DIM = 128
HG_VAL_DIM = 128
HG_WIDTH = HG_HEADS * HG_VAL_DIM
PEER_HEADS = 8
PEER_N_KEYS = 128
PEER_N_EXPERTS = PEER_N_KEYS * PEER_N_KEYS
PEER_HALF_DIM = 128
PEER_TOPK = 16
PEER_SLOTS = PEER_HEADS * PEER_TOPK
DEEPNORM_ALPHA = (2 * DEPTH) ** 0.25
LN_EPS = 1e-5
RMS_EPS = 1e-6
MASK_VALUE = -1e30
LOG2E = 1.4426950408889634
LB_FLOOR = 1e-30

LANES = 128
SUBLANES = 8
VMEM_LIMIT_BYTES = 60 * 1024 * 1024

FOX_PAIRS_PER_STEP = 1
HG_CHUNK = 16
HG_CUMSUM_ROWS = 128
W_ROW_PITCH = PEER_N_KEYS + SUBLANES
W_TOKEN_GROUP = 16

F32 = jnp.float32
BF16 = jnp.bfloat16


def _params(sem):
    return pltpu.CompilerParams(dimension_semantics=sem, vmem_limit_bytes=VMEM_LIMIT_BYTES)


def _dot(a, b):
    return jnp.dot(a, b, preferred_element_type=F32)


def _dot_nt(a, b):
    return lax.dot_general(a, b, (((1,), (1,)), ((), ())), preferred_element_type=F32)


def _layer_norm_rows(y, g, b):
    mu = jnp.mean(y, axis=-1, keepdims=True)
    d = y - mu
    var = jnp.mean(d * d, axis=-1, keepdims=True)
    return d * lax.rsqrt(var + LN_EPS) * g + b


def _project(h, wqkv_ref, wff_ref, whg_ref, qkv_ref, ff_ref, hg_ref):
    qkv_ref[...] = _dot(h, wqkv_ref[...]).astype(BF16)
    ff_ref[...] = _dot(h, wff_ref[...])
    hg_ref[...] = _dot(h, whg_ref[...])


def _inproj_kernel(h_ref, *refs):
    _project(h_ref[...], *refs)


def _ln_inproj_kernel(x_ref, g_ref, b_ref, wqkv_ref, wff_ref, whg_ref, h_ref, *out_refs):
    h = _layer_norm_rows(x_ref[...], g_ref[...], b_ref[...])
    h_ref[...] = h
    _project(h.astype(BF16), wqkv_ref, wff_ref, whg_ref, *out_refs)


def _inproj_call(src, wqkv, wff, whg, tm, ln=None):
    n, d = src.shape
    rows = lambda width: pl.BlockSpec((tm, width), lambda i: (i, 0))
    whole = lambda a: pl.BlockSpec(a.shape, lambda i: (0,) * a.ndim)
    weights = (wqkv, wff, whg)
    outs = [(w.shape[1], dt) for w, dt in zip(weights, (BF16, F32, F32))]
    args, in_specs = [src], [rows(d)]
    if ln is not None:
        ln = [a.reshape(1, d) for a in ln]
        args += ln
        in_specs += [whole(a) for a in ln]
        outs = [(d, F32)] + outs
    return pl.pallas_call(
        _inproj_kernel if ln is None else _ln_inproj_kernel,
        grid=(n // tm,),
        in_specs=in_specs + [whole(w) for w in weights],
        out_specs=[rows(width) for width, _ in outs],
        out_shape=[jax.ShapeDtypeStruct((n, width), dt) for width, dt in outs],
        compiler_params=_params(("parallel",)),
        name="in_proj",
    )(*args, *weights)


def _log_sigmoid(z):
    return jnp.minimum(z, 0.0) - jnp.log(1.0 + jnp.exp(-jnp.abs(z)))


def _fox_cumsum_kernel(ff_ref, fb_ref, c_ref):
    rows, t = ff_ref.shape
    lf = _log_sigmoid(ff_ref[...] + fb_ref[...]) * LOG2E
    r = lax.broadcasted_iota(jnp.int32, (LANES, LANES), 0)
    c = lax.broadcasted_iota(jnp.int32, (LANES, LANES), 1)
    upper = (r <= c).astype(F32).astype(BF16)
    carry = jnp.zeros((rows, 1), F32)
    for j in range(t // LANES):
        blk = lf[:, j * LANES:(j + 1) * LANES]
        cs = sum(_dot(piece.astype(BF16), upper) for piece in _split3(blk))
        cs = cs + carry
        c_ref[:, j * LANES:(j + 1) * LANES] = cs
        carry = cs[:, LANES - 1:LANES]


def _fox_cumsum_call(ff_t, fb_rows):
    rows, t = ff_t.shape
    tr = SUBLANES
    return pl.pallas_call(
        _fox_cumsum_kernel,
        grid=(rows // tr,),
        in_specs=[pl.BlockSpec((tr, t), lambda i: (i, 0)),
                  pl.BlockSpec((tr, 1), lambda i: (i, 0))],
        out_specs=pl.BlockSpec((tr, t), lambda i: (i, 0)),
        out_shape=jax.ShapeDtypeStruct((rows, t), F32),
        compiler_params=_params(("parallel",)),
        name="fox_cumsum",
    )(ff_t, fb_rows)


def _split3(x):
    hi = x.astype(BF16).astype(F32)
    r1 = x - hi
    mid = r1.astype(BF16).astype(F32)
    lo = (r1 - mid).astype(BF16).astype(F32)
    return hi, mid, lo


def _fox_kernel(q_ref, k_ref, v_ref, c_ref, o_ref, kx_ref, vx_ref, qe_ref, *, tq, wide, seq, pairs):
    qi = pl.program_id(2)
    lane = lax.broadcasted_iota(jnp.int32, (1, LANES), 1)
    data = [lane < FOX_HEAD_DIM, lane >= FOX_HEAD_DIM]
    spare = [FOX_HEAD_DIM, 0]
    heads = [(p, h) for p in range(pairs) for h in range(2)]
    slab = lambda p: slice(p * LANES, (p + 1) * LANES)

    sub8 = lax.broadcasted_iota(jnp.int32, (SUBLANES, LANES), 0)

    def bias_tile(h, pieces):
        rows8 = jnp.zeros((SUBLANES, LANES), F32)
        for j, piece in enumerate(pieces):
            rows8 = jnp.where(sub8 == j, piece, rows8)
        blocks = [jnp.zeros((spare[h], LANES), F32)] if spare[h] else []
        blocks += [rows8, jnp.zeros((LANES - spare[h] - SUBLANES, LANES), F32)]
        return jnp.transpose(jnp.concatenate(blocks, axis=0))

    @pl.when(qi == 0)
    def _():
        for n, (p, h) in enumerate(heads):
            for j in range(seq // LANES):
                rows = slice(j * LANES, (j + 1) * LANES)
                hi, mid, lo = _split3(c_ref[p, h:h + 1, rows])
                one = jnp.ones_like(hi)
                k_extra = bias_tile(h, (-hi, -mid, -lo, one, one, one))
                kx_ref[n, rows, :] = jnp.where(data[h], k_ref[rows, slab(p)].astype(F32), k_extra).astype(BF16)
                qe_ref[n, rows, :] = bias_tile(h, (one, one, one, hi, mid, lo)).astype(BF16)
                vx_ref[n, rows, :] = jnp.where(data[h], v_ref[rows, slab(p)].astype(F32), 1.0).astype(BF16)

    q0 = pl.multiple_of(qi * tq, tq)
    qx = []
    for n, (p, h) in enumerate(heads):
        q = q_ref[:, slab(p)].astype(F32)
        q_extra = qe_ref[n, pl.ds(q0, tq), :].astype(F32)
        qx.append((jnp.where(data[h], q, 0.0) + q_extra).astype(BF16))

    def block(k0, width, masked, carry):
        out = []
        for n in range(len(heads)):
            m_old, acc_old = carry[n]
            s = _dot_nt(qx[n], kx_ref[n, pl.ds(k0, width), :])
            if masked:
                row = lax.broadcasted_iota(jnp.int32, (tq, width), 0) + q0
                col = lax.broadcasted_iota(jnp.int32, (tq, width), 1) + k0
                s = jnp.where(col <= row, s, MASK_VALUE)
            m_new = jnp.maximum(m_old, jnp.max(s, axis=-1, keepdims=True))
            alpha = jnp.exp2(m_old - m_new)
            p = jnp.exp2(s - m_new).astype(BF16)
            acc_new = alpha * acc_old + _dot(p, vx_ref[n, pl.ds(k0, width), :])
            out.append((m_new, acc_new))
        return tuple(out)

    init = tuple((jnp.full((tq, 1), MASK_VALUE, F32), jnp.zeros((tq, LANES), F32)) for _ in heads)
    assert wide in (tq, 2 * tq)
    n_wide = q0 // wide
    carry = lax.fori_loop(
        0, n_wide, lambda j, c: block(pl.multiple_of(j * wide, wide), wide, False, c), init)
    if wide == tq:
        carry = block(q0, tq, True, carry)
    else:
        carry = lax.cond(q0 % wide == 0,
                         lambda c: block(q0, tq, True, c),
                         lambda c: block(pl.multiple_of(n_wide * wide, wide), wide, True, c),
                         carry)
    for p in range(pairs):
        res = []
        for h in range(2):
            acc = carry[2 * p + h][1]
            res.append(acc / pltpu.roll(acc, FOX_HEAD_DIM, axis=1))
        o_ref[:, slab(p)] = jnp.where(data[0], res[0], res[1]).astype(o_ref.dtype)


def _fox_call(qkv, c, batch, seq, tq, wide, pairs):
    steps = FOX_HEADS // 2 // pairs
    width = pairs * LANES
    return pl.pallas_call(
        functools.partial(_fox_kernel, tq=tq, wide=wide, seq=seq, pairs=pairs),
        grid=(batch, steps, seq // tq),
        in_specs=[pl.BlockSpec((None, tq, width), lambda b, p, i: (b, i, p)),
                  pl.BlockSpec((None, seq, width), lambda b, p, i: (b, 0, steps + p)),
                  pl.BlockSpec((None, seq, width), lambda b, p, i: (b, 0, 2 * steps + p)),
                  pl.BlockSpec((None, pairs, 2, seq), lambda b, p, i: (b, p, 0, 0))],
        out_specs=pl.BlockSpec((None, tq, width), lambda b, p, i: (b, i, p)),
        out_shape=jax.ShapeDtypeStruct((batch, seq, FOX_WIDTH), BF16),
        scratch_shapes=[pltpu.VMEM((2 * pairs, seq, LANES), BF16),
                        pltpu.VMEM((2 * pairs, seq, LANES), BF16),
                        pltpu.VMEM((2 * pairs, seq, LANES), BF16)],
        compiler_params=_params(("parallel", "parallel", "arbitrary")),
        name="fox_attention",
    )(qkv, qkv, qkv, c)


def _hgrn_kernel(q_ref, z_ref, v_ref, lbl_ref, o_ref, lf_ref, kk_ref, st_ref, *, seq):
    logits = lbl_ref[...]
    lg = logits[:DEPTH, :]
    mx = jnp.max(lg, axis=0, keepdims=True)
    ex = jnp.exp(lg - mx)
    probs = ex / jnp.sum(ex, axis=0, keepdims=True)
    sel = logits[DEPTH:DEPTH + 1, :]
    lb = jnp.zeros_like(sel)
    run = jnp.zeros_like(sel)
    for i in range(DEPTH):
        run = run + probs[i:i + 1, :]
        lb = jnp.where(sel == float(i), run - probs[0:1, :], lb)
    log_lb = jnp.log(jnp.maximum(lb, LB_FLOOR))
    log_1m = jnp.log(1.0 - lb)

    rr = lax.broadcasted_iota(jnp.int32, (HG_CUMSUM_ROWS, HG_CUMSUM_ROWS), 0)
    cc = lax.broadcasted_iota(jnp.int32, (HG_CUMSUM_ROWS, HG_CUMSUM_ROWS), 1)
    tri = ((rr // HG_CHUNK == cc // HG_CHUNK) & (cc <= rr)).astype(F32).astype(BF16)

    def prep(j, _):
        r0 = pl.multiple_of(j * HG_CUMSUM_ROWS, HG_CUMSUM_ROWS)
        z = z_ref[pl.ds(r0, HG_CUMSUM_ROWS), :]
        a = log_1m + _log_sigmoid(z)
        mm = jnp.maximum(log_lb, a)
        lf = (mm + jnp.log(1.0 + jnp.exp(-jnp.abs(log_lb - a)))) * LOG2E
        lf_ref[pl.ds(r0, HG_CUMSUM_ROWS), :] = sum(_dot(tri, piece.astype(BF16)) for piece in _split3(lf))
        kk_ref[pl.ds(r0, HG_CUMSUM_ROWS), :] = (1.0 - lb) * jax.nn.sigmoid(-z)
        return 0

    lax.fori_loop(0, seq // HG_CUMSUM_ROWS, prep, 0, unroll=2)
    st_ref[...] = jnp.zeros_like(st_ref)
    row8 = lax.broadcasted_iota(jnp.int32, (SUBLANES, 1), 0)

    def chunk(ci, _):
        r0 = pl.multiple_of(ci * HG_CHUNK, HG_CHUNK)
        b = lf_ref[pl.ds(r0, HG_CHUNK), :]
        q = q_ref[pl.ds(r0, HG_CHUNK), :]
        kk = kk_ref[pl.ds(r0, HG_CHUNK), :]
        v = v_ref[pl.ds(r0, HG_CHUNK), :]
        b_last = b[HG_CHUNK - 1:HG_CHUNK, :]
        qe = (q * jnp.exp2(b)).astype(BF16)
        kd = (kk * jnp.exp2(b_last - b)).astype(BF16)
        e_last = jnp.exp2(b_last)
        outs = []
        for h in range(HG_HEADS):
            sl = slice(h * HG_KEY_DIM, (h + 1) * HG_KEY_DIM)
            st = st_ref[h]
            o = _dot_nt(qe[:, sl], st.astype(BF16))
            bh, qh, kh, vh = b[:, sl], q[:, sl], kk[:, sl], v[:, sl]
            halves = []
            for half in range(HG_CHUNK // SUBLANES):
                rows = slice(half * SUBLANES, (half + 1) * SUBLANES)
                bt, qt, acc = bh[rows], qh[rows], o[rows]
                for s in range((half + 1) * SUBLANES):
                    diff = bt - bh[s:s + 1, :]
                    first_live = s - half * SUBLANES
                    if first_live > 0:
                        diff = jnp.where(row8 >= first_live, diff, MASK_VALUE)
                    a_col = jnp.sum(qt * kh[s:s + 1, :] * jnp.exp2(diff), axis=-1, keepdims=True)
                    acc = acc + a_col * vh[s:s + 1, :]
                halves.append(acc)
            o = jnp.concatenate(halves, axis=0)
            upd = lax.dot_general(vh.astype(BF16), kd[:, sl], (((0,), (0,)), ((), ())),
                                  preferred_element_type=F32)
            st_ref[h] = st * e_last[:, sl] + upd
            outs.append(o)
        o_ref[pl.ds(r0, HG_CHUNK), :] = jnp.concatenate(outs, axis=1).astype(o_ref.dtype)
        return 0

    lax.fori_loop(0, seq // HG_CHUNK, chunk, 0, unroll=4)


def _hgrn_call(hgp, lbl, batch, seq):
    return pl.pallas_call(
        functools.partial(_hgrn_kernel, seq=seq),
        grid=(batch,),
        in_specs=[pl.BlockSpec((None, seq, HG_WIDTH), lambda b: (b, 0, 0)),
                  pl.BlockSpec((None, seq, HG_WIDTH), lambda b: (b, 0, 1)),
                  pl.BlockSpec((None, seq, HG_WIDTH), lambda b: (b, 0, 2)),
                  pl.BlockSpec(lbl.shape, lambda b: (0, 0))],
        out_specs=pl.BlockSpec((None, seq, HG_WIDTH), lambda b: (b, 0, 0)),
        out_shape=jax.ShapeDtypeStruct((batch, seq, HG_WIDTH), BF16),
        scratch_shapes=[pltpu.VMEM((seq, HG_WIDTH), F32),
                        pltpu.VMEM((seq, HG_WIDTH), F32),
                        pltpu.VMEM((HG_HEADS, HG_VAL_DIM, HG_KEY_DIM), F32)],
        compiler_params=_params(("parallel",)),
        name="hgrn2",
    )(hgp, hgp, hgp, lbl)


def _mix_kernel(fox_ref, ohg_ref, gate_ref, h_ref, gf_ref, gh_ref, wo_ref, g1_ref, b1_ref,
                h1_ref, h1b_ref):
    fox = fox_ref[...].astype(F32)
    fox = fox * lax.rsqrt(jnp.mean(fox * fox, axis=-1, keepdims=True) + RMS_EPS) * gf_ref[...]
    ohg = ohg_ref[...].astype(F32)
    gh = gh_ref[...]
    gate = gate_ref[...]
    parts = [fox.astype(BF16)]
    for h in range(HG_HEADS):
        sl = slice(h * HG_VAL_DIM, (h + 1) * HG_VAL_DIM)
        o = ohg[:, sl]
        o = o * lax.rsqrt(jnp.mean(o * o, axis=-1, keepdims=True) + RMS_EPS) * gh[:, sl]
        g = gate[:, sl]
        parts.append((o * (g * jax.nn.sigmoid(g))).astype(BF16))
    cat = jnp.concatenate(parts, axis=1)
    mix = _dot(cat, wo_ref[...])
    h1 = _layer_norm_rows(DEEPNORM_ALPHA * h_ref[...] + mix, g1_ref[...], b1_ref[...])
    h1_ref[...] = h1
    h1b_ref[...] = h1.astype(BF16)


def _mix_call(fox_o, ohg, hgp, h, gf, gh, wo, g1, b1, tm):
    n, d = h.shape
    row = lambda a: a.reshape(1, -1)
    return pl.pallas_call(
        _mix_kernel,
        grid=(n // tm,),
        in_specs=[pl.BlockSpec((tm, FOX_WIDTH), lambda i: (i, 0)),
                  pl.BlockSpec((tm, HG_WIDTH), lambda i: (i, 0)),
                  pl.BlockSpec((tm, HG_WIDTH), lambda i: (i, 3)),
                  pl.BlockSpec((tm, d), lambda i: (i, 0)),
                  pl.BlockSpec((1, FOX_WIDTH), lambda i: (0, 0)),
                  pl.BlockSpec((1, HG_WIDTH), lambda i: (0, 0)),
                  pl.BlockSpec(wo.shape, lambda i: (0, 0)),
                  pl.BlockSpec((1, d), lambda i: (0, 0)),
                  pl.BlockSpec((1, d), lambda i: (0, 0))],
        out_specs=[pl.BlockSpec((tm, d), lambda i: (i, 0)),
                   pl.BlockSpec((tm, d), lambda i: (i, 0))],
        out_shape=[jax.ShapeDtypeStruct((n, d), F32), jax.ShapeDtypeStruct((n, d), BF16)],
        compiler_params=_params(("parallel",)),
        name="mix_out_ln1",
    )(fox_o, ohg, hgp, h, row(gf), row(gh), wo, row(g1), row(b1))


class _Network:
    def __init__(self):
        self.comparators = []
        self.outputs = []


def _net_merge(net, xs, ys):
    if not xs:
        return list(ys)
    if not ys:
        return list(xs)
    if len(xs) == 1 and len(ys) == 1:
        net.comparators.append((xs[0], ys[0]))
        return [xs[0], ys[0]]
    evens = _net_merge(net, xs[0::2], ys[0::2])
    odds = _net_merge(net, xs[1::2], ys[1::2])
    out = [evens[0]]
    for i, w in enumerate(odds):
        if i + 1 < len(evens):
            net.comparators.append((evens[i + 1], w))
            out += [evens[i + 1], w]
        else:
            out.append(w)
    return out + evens[len(odds) + 1:]


def _net_sort(net, ws):
    if len(ws) <= 1:
        return list(ws)
    mid = len(ws) // 2
    return _net_merge(net, _net_sort(net, ws[:mid]), _net_sort(net, ws[mid:]))


def _top_network(sizes, presorted, keep):
    net = _Network()
    lists, w = [], 0
    for s in sizes:
        ws = list(range(w, w + s))
        w += s
        lists.append(ws if presorted else _net_sort(net, ws))
    while len(lists) > 1:
        lists.sort(key=len)
        lists = [_net_merge(net, lists[0], lists[1])[:keep]] + lists[2:]
    net.outputs = lists[0][:keep]
    needed, kept = set(net.outputs), []
    for a, b in reversed(net.comparators):
        if a in needed or b in needed:
            kept.append((a, b, a in needed, b in needed))
            needed.update((a, b))
    net.comparators = kept[::-1]
    return net


def _run_network(net, wires):
    wires = list(wires)
    for a, b, need_a, need_b in net.comparators:
        (va, ia), (vb, ib) = wires[a], wires[b]
        a_first = (va > vb) | ((va == vb) & (ia < ib))
        if need_a:
            wires[a] = (jnp.maximum(va, vb), jnp.where(a_first, ia, ib))
        if need_b:
            wires[b] = (jnp.minimum(va, vb), jnp.where(a_first, ib, ia))
    return [wires[w] for w in net.outputs]


def _pair_lists():
    ok = lambda j1, j2: (j1 + 1) * (j2 + 1) <= PEER_TOPK
    lists = []
    for d in range(PEER_TOPK):
        row = [(d, j2) for j2 in range(d, PEER_TOPK) if ok(d, j2)]
        col = [(j1, d) for j1 in range(d + 1, PEER_TOPK) if ok(j1, d)]
        lists += [l for l in (row, col) if l]
    return lists


KEY_BITS = PEER_N_KEYS.bit_length() - 1
FLAT_SHIFT = 2 * KEY_BITS
_KEY_NET = _top_network([PEER_TOPK] * (PEER_N_KEYS // PEER_TOPK), presorted=False, keep=PEER_TOPK)
_PAIR_LISTS = _pair_lists()
_PAIR_NET = _top_network([len(l) for l in _PAIR_LISTS], presorted=True, keep=PEER_TOPK)


SLAB_TOKENS = SUBLANES * LANES
SLAB_ROWS = SUBLANES * W_ROW_PITCH


def _peerq_kernel(h_ref, wq_ref, keys_ref, a_ref, b_ref, g_ref,
                  sc_ref, tv_ref, ti_ref, oa_ref, ob_ref, og_ref):
    n_lists = 2 * PEER_HEADS
    q = _dot(h_ref[...], wq_ref[...]).astype(BF16)
    for hp in range(n_lists):
        s = _dot_nt(keys_ref[hp], q[:, hp * PEER_HALF_DIM:(hp + 1) * PEER_HALF_DIM])
        for g in range(SUBLANES):
            sc_ref[pl.ds(hp * SLAB_ROWS + g * W_ROW_PITCH, PEER_N_KEYS), :] = s[:, g * LANES:(g + 1) * LANES]

    def keys_top(hp, _):
        base = hp * SLAB_ROWS
        wires = [(sc_ref[pl.ds(base + n, SUBLANES, stride=W_ROW_PITCH), :],
                  jnp.full((SUBLANES, LANES), n, jnp.int32)) for n in range(PEER_N_KEYS)]
        for j, (v, i) in enumerate(_run_network(_KEY_NET, wires)):
            tv_ref[hp * PEER_TOPK + j] = v
            ti_ref[hp * PEER_TOPK + j] = i
        return 0

    lax.fori_loop(0, n_lists, keys_top, 0)

    def pairs_top(h, _):
        r1 = 2 * h * PEER_TOPK
        r2 = r1 + PEER_TOPK
        s1 = [tv_ref[r1 + j] for j in range(PEER_TOPK)]
        s2 = [tv_ref[r2 + j] for j in range(PEER_TOPK)]
        e1 = [ti_ref[r1 + j] * PEER_N_KEYS for j in range(PEER_TOPK)]
        i2 = [ti_ref[r2 + j] for j in range(PEER_TOPK)]
        wires = [(s1[j1] + s2[j2], e1[j1] + i2[j2] + ((j1 * PEER_TOPK + j2) << FLAT_SHIFT))
                 for lst in _PAIR_LISTS for (j1, j2) in lst]
        top = _run_network(_PAIR_NET, wires)
        es = [jnp.exp(v - top[0][0]) for v, _ in top]
        total = es[0]
        for e in es[1:]:
            total = total + e
        inv = 1.0 / total
        for k, ((_, key), e) in enumerate(zip(top, es)):
            expert = key & ((1 << FLAT_SHIFT) - 1)
            rows = pl.ds(h * PEER_TOPK + k, SUBLANES, stride=W_ROW_PITCH)
            oa_ref[rows, :] = lax.shift_right_logical(expert, KEY_BITS)
            ob_ref[rows, :] = expert & (PEER_N_KEYS - 1)
            og_ref[rows, :] = e * inv
        return 0

    lax.fori_loop(0, PEER_HEADS, pairs_top, 0)
    for g in range(SUBLANES):
        rows = pl.ds(g * W_ROW_PITCH, PEER_SLOTS)
        toks = slice(g * LANES, (g + 1) * LANES)
        a_ref[toks, :] = jnp.transpose(oa_ref[rows, :])
        b_ref[toks, :] = jnp.transpose(ob_ref[rows, :])
        g_ref[toks, :] = jnp.transpose(og_ref[rows, :])


def _peerq_call(h1b, wq, keys):
    n, d = h1b.shape
    tb = SLAB_TOKENS
    slot = lambda dt: jax.ShapeDtypeStruct((n, PEER_SLOTS), dt)
    return pl.pallas_call(
        _peerq_kernel,
        grid=(n // tb,),
        in_specs=[pl.BlockSpec((tb, d), lambda i: (i, 0)),
                  pl.BlockSpec(wq.shape, lambda i: (0, 0)),
                  pl.BlockSpec(keys.shape, lambda i: (0, 0, 0))],
        out_specs=[pl.BlockSpec((tb, PEER_SLOTS), lambda i: (i, 0))] * 3,
        out_shape=[slot(jnp.int32), slot(jnp.int32), slot(F32)],
        scratch_shapes=[pltpu.VMEM((2 * PEER_HEADS * SLAB_ROWS, LANES), F32),
                        pltpu.VMEM((2 * PEER_HEADS * PEER_TOPK, SUBLANES, LANES), F32),
                        pltpu.VMEM((2 * PEER_HEADS * PEER_TOPK, SUBLANES, LANES), jnp.int32),
                        pltpu.VMEM((SLAB_ROWS, LANES), jnp.int32),
                        pltpu.VMEM((SLAB_ROWS, LANES), jnp.int32),
                        pltpu.VMEM((SLAB_ROWS, LANES), F32)],
        compiler_params=_params(("parallel",)),
        name="peer_retrieve",
    )(h1b, wq, keys)


def _peer_act_kernel(x_ref, u_ref, a_ref, b_ref, o_ref, *, ec):
    j = pl.program_id(1)

    @pl.when(j == 0)
    def _():
        o_ref[...] = jnp.zeros_like(o_ref)

    act = _dot_nt(x_ref[...], u_ref[...])
    a_idx = a_ref[...]
    b_idx = b_ref[...]
    acc = o_ref[...]
    groups = ec // PEER_N_KEYS
    for g in range(groups):
        i1 = j * groups + g
        picked = jnp.take_along_axis(act[:, g * PEER_N_KEYS:(g + 1) * PEER_N_KEYS], b_idx, axis=1)
        acc = jnp.where(a_idx == i1, picked, acc)
    o_ref[...] = acc


def _peer_act_call(h1b, u, a_idx, b_idx, tb, ec):
    n, d = h1b.shape
    return pl.pallas_call(
        functools.partial(_peer_act_kernel, ec=ec),
        grid=(n // tb, PEER_N_EXPERTS // ec),
        in_specs=[pl.BlockSpec((tb, d), lambda i, j: (i, 0)),
                  pl.BlockSpec((ec, d), lambda i, j: (j, 0)),
                  pl.BlockSpec((tb, PEER_SLOTS), lambda i, j: (i, 0)),
                  pl.BlockSpec((tb, PEER_SLOTS), lambda i, j: (i, 0))],
        out_specs=pl.BlockSpec((tb, PEER_SLOTS), lambda i, j: (i, 0)),
        out_shape=jax.ShapeDtypeStruct((n, PEER_SLOTS), F32),
        compiler_params=_params(("parallel", "arbitrary")),
        name="peer_expert_in",
    )(h1b, u, a_idx, b_idx)


def _peer_w_kernel(act_ref, g_ref, a_ref, b_ref, w_ref, wt_ref, w3_ref):
    tb = act_ref.shape[0]
    act = act_ref[...]
    gelu = 0.5 * act * (1.0 + lax.erf(act * (2.0 ** -0.5)))
    wt_ref[...] = g_ref[...] * gelu
    sub = lax.broadcasted_iota(jnp.int32, (PEER_N_KEYS, PEER_SLOTS), 0)

    def build_pair(w3_ref, g0, j):
        pts, qts = [], []
        for t in (g0 + j, g0 + j + 1):
            pts.append(jnp.where(sub == a_ref[t:t + 1, :], wt_ref[t:t + 1, :], 0.0).astype(BF16))
            qts.append(jnp.where(sub == b_ref[t:t + 1, :], 1.0, 0.0).astype(BF16))
        zero = jnp.zeros_like(qts[0])
        rhs_t = jnp.concatenate([jnp.concatenate([qts[0], zero], axis=1),
                                 jnp.concatenate([zero, qts[1]], axis=1)], axis=0)
        both = _dot_nt(jnp.concatenate(pts, axis=1), rhs_t)
        for k in range(2):
            r0 = (j + k) * W_ROW_PITCH
            w3_ref[r0:r0 + PEER_N_KEYS, :] = both[:, k * PEER_N_KEYS:(k + 1) * PEER_N_KEYS]

    def relayout(w3_ref, g0, i1):
        rows = w3_ref[pl.ds(i1, W_TOKEN_GROUP, stride=W_ROW_PITCH), :]
        w_ref[g0:g0 + W_TOKEN_GROUP, i1 * PEER_N_KEYS:(i1 + 1) * PEER_N_KEYS] = rows.astype(BF16)

    for g0 in range(0, tb, W_TOKEN_GROUP):
        for j in range(0, W_TOKEN_GROUP, 2):
            build_pair(w3_ref, g0, j)
        for i1 in range(PEER_N_KEYS):
            relayout(w3_ref, g0, i1)


def _peer_w_call(act, gate, a_idx, b_idx, tb):
    n = act.shape[0]
    spec = pl.BlockSpec((tb, PEER_SLOTS), lambda i: (i, 0))
    return pl.pallas_call(
        _peer_w_kernel,
        grid=(n // tb,),
        in_specs=[spec, spec, spec, spec],
        out_specs=pl.BlockSpec((tb, PEER_N_EXPERTS), lambda i: (i, 0)),
        out_shape=jax.ShapeDtypeStruct((n, PEER_N_EXPERTS), BF16),
        scratch_shapes=[pltpu.VMEM((tb, PEER_SLOTS), F32),
                        pltpu.VMEM((W_TOKEN_GROUP * W_ROW_PITCH, PEER_N_KEYS), F32)],
        compiler_params=_params(("parallel",)),
        name="peer_expert_weights",
    )(act, gate, a_idx, b_idx)


def _peer_out_kernel(w_ref, v_ref, h1_ref, h1b_ref, p_ref, gw_ref, pw_ref, g2_ref, b2_ref,
                     h2_ref, h2b_ref, acc_ref):
    j = pl.program_id(1)

    @pl.when(j == 0)
    def _():
        acc_ref[...] = jnp.zeros_like(acc_ref)

    acc_ref[...] += _dot(w_ref[...], v_ref[...])

    @pl.when(j == pl.num_programs(1) - 1)
    def _():
        gate = jax.nn.sigmoid(_dot(h1b_ref[...], gw_ref[...]))
        ple = gate * _dot(p_ref[...].astype(BF16), pw_ref[...])
        y = DEEPNORM_ALPHA * h1_ref[...] + acc_ref[...] + ple
        h2 = _layer_norm_rows(y, g2_ref[...], b2_ref[...])
        h2_ref[...] = h2
        h2b_ref[...] = h2.astype(BF16)


def _peer_out_call(w, v, h1, h1b, p, gw, pw, g2, b2, tm, tk):
    n, d = h1.shape
    row = lambda a: a.reshape(1, -1)
    return pl.pallas_call(
        _peer_out_kernel,
        grid=(n // tm, PEER_N_EXPERTS // tk),
        in_specs=[pl.BlockSpec((tm, tk), lambda i, j: (i, j)),
                  pl.BlockSpec((tk, d), lambda i, j: (j, 0)),
                  pl.BlockSpec((tm, d), lambda i, j: (i, 0)),
                  pl.BlockSpec((tm, d), lambda i, j: (i, 0)),
                  pl.BlockSpec((tm, PLE_DIM), lambda i, j: (i, 0)),
                  pl.BlockSpec(gw.shape, lambda i, j: (0, 0)),
                  pl.BlockSpec(pw.shape, lambda i, j: (0, 0)),
                  pl.BlockSpec((1, d), lambda i, j: (0, 0)),
                  pl.BlockSpec((1, d), lambda i, j: (0, 0))],
        out_specs=[pl.BlockSpec((tm, d), lambda i, j: (i, 0)),
                   pl.BlockSpec((tm, d), lambda i, j: (i, 0))],
        out_shape=[jax.ShapeDtypeStruct((n, d), F32), jax.ShapeDtypeStruct((n, d), BF16)],
        scratch_shapes=[pltpu.VMEM((tm, d), F32)],
        compiler_params=_params(("parallel", "arbitrary")),
        name="peer_out_ple_ln2",
    )(w, v, h1, h1b, p, gw, pw, row(g2), row(b2))


def _tiles(n, seq):
    pick = lambda pref, total: pref if total % pref == 0 else total
    return dict(
        rows=pick(512, n),
        tq=pick(512, seq), wide=pick(1024, seq),
        act_tb=pick(1024, n), act_ec=4096,
        w_tb=pick(256, n),
        out_tm=pick(1024, n), out_tk=2048,
    )


def kernel(x, p, emb_ln_g, emb_ln_b, w_in, fox_fb, fox_norm_g, hg_lb_logits, hg_norm_g, w_out,
           ln1_g, ln1_b, peer_wq, peer_keys, peer_u, peer_v, ple_gate_w, ple_w, ln2_g, ln2_b):
    batch, seq, d = x.shape
    n = batch * seq
    assert n % SLAB_TOKENS == 0, "token count must be a multiple of the retrieval block"
    tl = _tiles(n, seq)
    scale = FOX_HEAD_DIM ** -0.5 * LOG2E
    o_ff = 3 * FOX_WIDTH
    o_hg = o_ff + FOX_HEADS
    for i in range(DEPTH):
        wi = w_in[i]
        wqkv = jnp.concatenate([wi[:, :FOX_WIDTH] * scale, wi[:, FOX_WIDTH:o_ff]], axis=1).astype(BF16)
        wff = jnp.pad(wi[:, o_ff:o_hg], ((0, 0), (0, LANES - FOX_HEADS))).astype(BF16)
        whg = wi[:, o_hg:].astype(BF16)
        if i == 0:
            h, qkv, ff, hgp = _inproj_call(x.reshape(n, d), wqkv, wff, whg, tl["rows"],
                                           ln=(emb_ln_g, emb_ln_b))
        else:
            qkv, ff, hgp = _inproj_call(hb, wqkv, wff, whg, tl["rows"])
        ff_t = ff[:, :FOX_HEADS].reshape(batch, seq, FOX_HEADS).transpose(0, 2, 1).reshape(batch * FOX_HEADS, seq)
        fb_rows = jnp.tile(fox_fb[i], batch).reshape(batch * FOX_HEADS, 1)
        c = _fox_cumsum_call(ff_t, fb_rows).reshape(batch, FOX_HEADS // 2, 2, seq)
        fox_o = _fox_call(qkv.reshape(batch, seq, 3 * FOX_WIDTH), c, batch, seq, tl["tq"], tl["wide"], FOX_PAIRS_PER_STEP)
        lbl = jnp.concatenate([hg_lb_logits.astype(F32), jnp.full((1, HG_WIDTH), float(i), F32)], axis=0)
        ohg = _hgrn_call(hgp.reshape(batch, seq, 4 * HG_WIDTH), lbl, batch, seq)
        h1, h1b = _mix_call(fox_o.reshape(n, FOX_WIDTH), ohg.reshape(n, HG_WIDTH), hgp, h,
                            fox_norm_g[i], hg_norm_g[i], w_out[i].astype(BF16), ln1_g[i], ln1_b[i], tl["rows"])
        keys = peer_keys[i].reshape(PEER_HEADS * 2, PEER_N_KEYS, PEER_HALF_DIM).astype(BF16)
        a_idx, b_idx, gate = _peerq_call(h1b, peer_wq[i].astype(BF16), keys)
        act = _peer_act_call(h1b, peer_u[i].astype(BF16), a_idx, b_idx, tl["act_tb"], tl["act_ec"])
        w = _peer_w_call(act, gate, a_idx, b_idx, tl["w_tb"])
        h, hb = _peer_out_call(w, peer_v[i].astype(BF16), h1, h1b, p[i].reshape(n, PLE_DIM),
                               ple_gate_w[i].astype(BF16), ple_w[i].astype(BF16), ln2_g[i], ln2_b[i],
                               tl["out_tm"], tl["out_tk"])
    return h.reshape(batch, seq, d)
```
